```python
import math
import jax, jax.numpy as jnp
from jax import lax
import numpy as np

D_MODEL = 4096
BATCH = 4
SEQ = 2048
DEPTH = 2
DEC_BATCH = 128
DEC_SEQ = 1
PAST_LEN = 16384
PAGE_SIZE = 128

N_BRANCH = 3
BRANCH_DIM = D_MODEL // 2
RWKV_HEAD = 64
RWKV_HEADS = BRANCH_DIM // RWKV_HEAD
RWKV_LORA_W = max(32, int(round(1.8 * BRANCH_DIM ** 0.5 / 32)) * 32)
RWKV_LORA_A = max(32, int(round(1.8 * BRANCH_DIM ** 0.5 / 32)) * 32)
RWKV_LORA_G = max(32, int(round(0.6 * BRANCH_DIM ** 0.8 / 32)) * 32)
RWKV_COLS = 3 * BRANCH_DIM + RWKV_LORA_W + RWKV_LORA_A + RWKV_LORA_G
RWKV_LN_EPS = 64e-5
SSM_HEADDIM = 64
SSM_HEADS = BRANCH_DIM // SSM_HEADDIM
SSM_GROUPS = 4
SSM_STATE = 128
SSM_CONV = 4
SSM_CHUNK = 64
CONV_CH = BRANCH_DIM + 2 * SSM_GROUPS * SSM_STATE
SSM_COLS = BRANCH_DIM + CONV_CH + SSM_HEADS
GLA_HEADS = 4
GLA_KDIM = BRANCH_DIM // 2
GLA_VDIM = BRANCH_DIM
GLA_DK = GLA_KDIM // GLA_HEADS
GLA_DV = GLA_VDIM // GLA_HEADS
GLA_LORA = 16
GLA_TAU = 16.0
GLA_CHUNK = 64
GLA_COLS = 2 * GLA_KDIM + 2 * GLA_VDIM + GLA_LORA
IN_COLS = RWKV_COLS + SSM_COLS + GLA_COLS + N_BRANCH * D_MODEL
D_FF = 4 * D_MODEL
PLE_DIM = 256
EPS = 1e-6

kernel_name = "hybrid_rwkv7_mamba2_gla_decode_step"


def _rms(x, eps=EPS):
    xf = x.astype(jnp.float32)
    return xf * lax.rsqrt(jnp.mean(xf * xf, axis=-1, keepdims=True) + eps)


def rms_norm(x, g):
    return (_rms(x) * g.astype(jnp.float32)).astype(x.dtype)


def causal_conv(u, buf, w, b):
    L = u.shape[1]
    cat = jnp.concatenate([buf.astype(u.dtype), u], axis=1)
    out = b + sum(cat[:, j:j + L] * w[j] for j in range(w.shape[0]))
    return out, cat[:, L:]


def _pad_chunks(t, c):
    L = t.shape[1]
    n = -(-L // c)
    t = jnp.pad(t, [(0, 0), (0, n * c - L)] + [(0, 0)] * (t.ndim - 2))
    return t.reshape(t.shape[0], n, c, *t.shape[2:])


def rwkv7_scan(r, log_w, k, v, a, b, s0):
    def step(s, inp):
        r_t, lw_t, k_t, v_t, a_t, b_t = inp
        sa = jnp.einsum('bhvk,bhk->bhv', s, a_t)
        s = (s * jnp.exp(lw_t)[:, :, None, :] + sa[..., None] * b_t[:, :, None, :]
             + v_t[..., None] * k_t[:, :, None, :])
        return s, jnp.einsum('bhvk,bhk->bhv', s, r_t)
    xs = tuple(jnp.moveaxis(t, 1, 0) for t in (r, log_w, k, v, a, b))
    s_t, o = lax.scan(step, s0.astype(jnp.float32), xs)
    return jnp.moveaxis(o, 0, 1), s_t


def rwkv7_mix(u, shift_prev, s0, mu, w0, w2, a0, a2, g2, k_k, k_a, r_k, ln_w, ln_b):
    Bt, L, _ = u.shape
    prev = jnp.concatenate([shift_prev[:, None].astype(u.dtype), u[:, :-1]], axis=1)
    xs = u + (prev - u) * mu
    new_shift = u[:, -1]
    r, k, v, wl, al, gl = jnp.split(
        xs, [BRANCH_DIM, 2 * BRANCH_DIM, 3 * BRANCH_DIM,
             3 * BRANCH_DIM + RWKV_LORA_W, 3 * BRANCH_DIM + RWKV_LORA_W + RWKV_LORA_A], axis=-1)
    w = -jax.nn.softplus(-(w0 + jnp.tanh(wl) @ w2)) - 0.5
    log_w = -jnp.exp(w)
    a = jax.nn.sigmoid(a0 + al @ a2)
    g = jax.nn.sigmoid(gl) @ g2
    hd = lambda t: t.reshape(Bt, L, RWKV_HEADS, RWKV_HEAD)
    kk = hd(k * k_k)
    kk = kk / jnp.maximum(jnp.sqrt(jnp.sum(kk * kk, axis=-1, keepdims=True)), 1e-12)
    k = k * (1.0 + (a - 1.0) * k_a)
    r, k, v, a, log_w = hd(r), hd(k), hd(v), hd(a), hd(log_w)
    o, s_t = rwkv7_scan(r, log_w, k, v, -kk, kk * a, s0)
    mean = jnp.mean(o, axis=-1, keepdims=True)
    var = jnp.mean(jnp.square(o - mean), axis=-1, keepdims=True)
    o = ((o - mean) * lax.rsqrt(var + RWKV_LN_EPS)).reshape(Bt, L, BRANCH_DIM) * ln_w + ln_b
    bonus = jnp.sum(r * k * r_k, axis=-1, keepdims=True) * v
    o = (o + bonus.reshape(Bt, L, BRANCH_DIM)) * g
    return o, new_shift, s_t


def ssd_chunked(x, dt, a, bm, cm, h0):
    Bt, L = x.shape[:2]
    c = min(SSM_CHUNK, L)
    xdt = _pad_chunks(x * dt[..., None], c)
    la = _pad_chunks(dt * a, c)
    bm, cm = _pad_chunks(bm, c), _pad_chunks(cm, c)
    cum = jnp.cumsum(la, axis=2)
    seg = cum[:, :, :, None] - cum[:, :, None, :]
    causal = jnp.tril(jnp.ones((c, c), dtype=bool))[:, :, None, None]
    lmat = jnp.exp(jnp.where(causal, seg, -jnp.inf))
    cb = jnp.einsum('bctgn,bcsgn->bctsg', cm, bm)
    y_diag = jnp.einsum('bctsg,bctsge,bcsgep->bctgep', cb, lmat, xdt)
    decay_to_end = jnp.exp(cum[:, :, -1:] - cum)
    chunk_state = jnp.einsum('bcsgn,bcsge,bcsgep->bcgepn', bm, decay_to_end, xdt)
    chunk_decay = jnp.exp(cum[:, :, -1])

    def step(h, inp):
        st, dec = inp
        return h * dec[..., None, None] + st, h
    h_t, h_prev = lax.scan(step, h0.astype(jnp.float32),
                           (jnp.moveaxis(chunk_state, 1, 0), jnp.moveaxis(chunk_decay, 1, 0)))
    h_prev = jnp.moveaxis(h_prev, 0, 1)
    y_off = jnp.einsum('bctgn,bcgepn,bctge->bctgep', cm, h_prev, jnp.exp(cum))
    n = cum.shape[1]
    y = (y_diag + y_off).reshape(Bt, n * c, *x.shape[2:])[:, :L]
    return y, h_t


def mamba2_mix(u, conv_buf, h0, conv_w, conv_b, dt_bias, a_log, d_skip, norm_w):
    Bt, L, _ = u.shape
    e = SSM_HEADS // SSM_GROUPS
    z, xbc, dt = jnp.split(u, [BRANCH_DIM, BRANCH_DIM + CONV_CH], axis=-1)
    xbc, new_buf = causal_conv(xbc, conv_buf, conv_w, conv_b)
    xbc = jax.nn.silu(xbc)
    xs, bm, cm = jnp.split(xbc, [BRANCH_DIM, BRANCH_DIM + SSM_GROUPS * SSM_STATE], axis=-1)
    xs = xs.reshape(Bt, L, SSM_GROUPS, e, SSM_HEADDIM)
    bm = bm.reshape(Bt, L, SSM_GROUPS, SSM_STATE)
    cm = cm.reshape(Bt, L, SSM_GROUPS, SSM_STATE)
    dt = jax.nn.softplus(dt + dt_bias).reshape(Bt, L, SSM_GROUPS, e)
    a = -jnp.exp(a_log.astype(jnp.float32)).reshape(SSM_GROUPS, e)
    y, h_t = ssd_chunked(xs, dt, a, bm, cm,
                         h0.reshape(Bt, SSM_GROUPS, e, SSM_HEADDIM, SSM_STATE))
    y = y + d_skip.reshape(SSM_GROUPS, e, 1) * xs
    y = y.reshape(Bt, L, BRANCH_DIM) * jax.nn.silu(z)
    y = _rms(y.reshape(Bt, L, SSM_GROUPS, BRANCH_DIM // SSM_GROUPS)).reshape(Bt, L, BRANCH_DIM) * norm_w
    return y, new_buf, h_t.reshape(Bt, SSM_HEADS, SSM_HEADDIM, SSM_STATE)


def gla_chunked(q, k, v, lg, s0):
    Bt, L = q.shape[:2]
    c = min(GLA_CHUNK, L)
    q, k, v, lg = (_pad_chunks(t, c) for t in (q, k, v, lg))
    b = jnp.cumsum(lg, axis=2)
    b_last = b[:, :, -1:]
    q_in = q * jnp.exp(b)
    k_in = k * jnp.exp(-b)
    k_end = k * jnp.exp(b_last - b)
    causal = jnp.tril(jnp.ones((c, c), dtype=bool))
    att = jnp.where(causal, jnp.einsum('bcthk,bcshk->bchts', q_in, k_in), 0.0)
    o_intra = jnp.einsum('bchts,bcshv->bcthv', att, v)
    chunk_state = jnp.einsum('bcshk,bcshv->bchkv', k_end, v)
    chunk_decay = jnp.exp(b_last[:, :, 0])

    def step(s, inp):
        st, dec = inp
        return s * dec[..., None] + st, s
    s_t, s_prev = lax.scan(step, s0.astype(jnp.float32),
                           (jnp.moveaxis(chunk_state, 1, 0), jnp.moveaxis(chunk_decay, 1, 0)))
    s_prev = jnp.moveaxis(s_prev, 0, 1)
    o = o_intra + jnp.einsum('bcthk,bchkv->bcthv', q_in, s_prev)
    n = o.shape[1]
    return o.reshape(Bt, n * c, *o.shape[3:])[:, :L], s_t


def gla_mix(u, s0, f_up, f_bias, norm_w):
    Bt, L, _ = u.shape
    q, k, v, g, fl = jnp.split(
        u, [GLA_KDIM, 2 * GLA_KDIM, 2 * GLA_KDIM + GLA_VDIM, 2 * GLA_KDIM + 2 * GLA_VDIM], axis=-1)
    lg = jax.nn.log_sigmoid(fl @ f_up + f_bias) / GLA_TAU
    q = q.reshape(Bt, L, GLA_HEADS, GLA_DK) * (GLA_DK ** -0.5)
    k = k.reshape(Bt, L, GLA_HEADS, GLA_DK)
    v = v.reshape(Bt, L, GLA_HEADS, GLA_DV)
    lg = lg.reshape(Bt, L, GLA_HEADS, GLA_DK)
    o, s_t = gla_chunked(q, k, v, lg, s0)
    o = (_rms(o) * norm_w).reshape(Bt, L, GLA_VDIM) * jax.nn.silu(g)
    return o, s_t


def decoder_layer(x, ple, s_rwkv, s_shift, s_ssm, s_conv, s_gla,
                  norm_mix, w_in, rw_mu, rw_w0, rw_w2, rw_a0, rw_a2, rw_g2, rw_kk, rw_ka, rw_rk,
                  rw_ln_w, rw_ln_b, ssm_conv_w, ssm_conv_b, ssm_dt_bias, ssm_a_log, ssm_d, ssm_norm,
                  gla_f_up, gla_f_bias, gla_norm, w_branch, w_out, norm_ffn, w_ff1, w_ff2,
                  norm_ple, w_ple_gate, w_ple_proj):
    h = rms_norm(x, norm_mix)
    u = (h @ w_in).astype(jnp.float32)
    o1 = RWKV_COLS
    o2 = o1 + SSM_COLS
    o3 = o2 + GLA_COLS
    u_rw, u_ssm, u_gla, u_gate = jnp.split(u, [o1, o2, o3], axis=-1)
    y_rw, new_shift, new_rwkv = rwkv7_mix(u_rw, s_shift, s_rwkv, rw_mu, rw_w0, rw_w2, rw_a0, rw_a2,
                                          rw_g2, rw_kk, rw_ka, rw_rk, rw_ln_w, rw_ln_b)
    y_ssm, new_conv, new_ssm = mamba2_mix(u_ssm, s_conv, s_ssm, ssm_conv_w, ssm_conv_b, ssm_dt_bias,
                                          ssm_a_log, ssm_d, ssm_norm)
    y_gla, new_gla = gla_mix(u_gla, s_gla, gla_f_up, gla_f_bias, gla_norm)
    gates = jax.nn.sigmoid(u_gate).reshape(*u_gate.shape[:-1], N_BRANCH, D_MODEL).astype(x.dtype)
    merged = sum(gates[..., j, :] * (o.astype(x.dtype) @ w_branch[j])
                 for j, o in enumerate((y_rw, y_ssm, y_gla)))
    x = x + merged @ w_out
    hf = rms_norm(x, norm_ffn)
    x = x + jnp.square(jax.nn.relu(hf @ w_ff1)) @ w_ff2
    x = x + (ple @ w_ple_proj) * jax.nn.sigmoid(rms_norm(x, norm_ple) @ w_ple_gate)
    return x, (new_rwkv, new_shift, new_ssm, new_conv, new_gla)


def setup_inputs(seed: int = 0) -> dict:
    key = jax.random.key(seed)
    keys = iter(jax.random.split(key, 64))
    nrm = lambda shape, s=1.0: jax.random.normal(next(keys), shape, jnp.float32) * s
    uni = lambda shape, lo, hi: jax.random.uniform(next(keys), shape, jnp.float32, lo, hi)
    gain = lambda shape: 1.0 + nrm(shape, 0.02)
    dt0 = jnp.exp(uni((DEPTH, SSM_HEADS), math.log(1e-3), math.log(1e-1)))
    return {
        "x_prompt": nrm((BATCH, SEQ, D_MODEL)),
        "x_sample": nrm((DEC_BATCH, DEC_SEQ, D_MODEL)),
        "p_prompt": nrm((DEPTH, BATCH, SEQ, PLE_DIM)),
        "p_sample": nrm((DEPTH, DEC_BATCH, DEC_SEQ, PLE_DIM)),
        "state_rwkv": nrm((DEPTH, DEC_BATCH, RWKV_HEADS, RWKV_HEAD, RWKV_HEAD), 0.3),
        "state_shift": nrm((DEPTH, DEC_BATCH, RWKV_COLS)),
        "state_ssm": nrm((DEPTH, DEC_BATCH, SSM_HEADS, SSM_HEADDIM, SSM_STATE), 0.3),
        "state_conv": nrm((DEPTH, DEC_BATCH, SSM_CONV - 1, CONV_CH)),
        "state_gla": nrm((DEPTH, DEC_BATCH, GLA_HEADS, GLA_DK, GLA_DV), 0.3),
        "norm_mix": gain((DEPTH, D_MODEL)),
        "w_in": nrm((DEPTH, D_MODEL, IN_COLS), D_MODEL ** -0.5),
        "rw_mu": uni((DEPTH, RWKV_COLS), 0.0, 1.0),
        "rw_w0": uni((DEPTH, BRANCH_DIM), -6.0, 0.0),
        "rw_w2": nrm((DEPTH, RWKV_LORA_W, BRANCH_DIM), 0.1),
        "rw_a0": nrm((DEPTH, BRANCH_DIM), 0.1),
        "rw_a2": nrm((DEPTH, RWKV_LORA_A, BRANCH_DIM), 0.1),
        "rw_g2": nrm((DEPTH, RWKV_LORA_G, BRANCH_DIM), RWKV_LORA_G ** -0.5),
        "rw_kk": 0.85 + nrm((DEPTH, BRANCH_DIM), 0.05),
        "rw_ka": 1.0 + nrm((DEPTH, BRANCH_DIM), 0.05),
        "rw_rk": nrm((DEPTH, RWKV_HEADS, RWKV_HEAD), 0.1),
        "rw_ln_w": gain((DEPTH, BRANCH_DIM)),
        "rw_ln_b": nrm((DEPTH, BRANCH_DIM), 0.02),
        "ssm_conv_w": nrm((DEPTH, SSM_CONV, CONV_CH), SSM_CONV ** -0.5),
        "ssm_conv_b": nrm((DEPTH, CONV_CH), 0.02),
        "ssm_dt_bias": dt0 + jnp.log(-jnp.expm1(-dt0)),
        "ssm_a_log": jnp.log(uni((DEPTH, SSM_HEADS), 1.0, 16.0)),
        "ssm_d": 1.0 + nrm((DEPTH, SSM_HEADS), 0.1),
        "ssm_norm": gain((DEPTH, BRANCH_DIM)),
        "gla_f_up": nrm((DEPTH, GLA_LORA, GLA_KDIM), GLA_LORA ** -0.5),
        "gla_f_bias": nrm((DEPTH, GLA_KDIM), 0.5),
        "gla_norm": gain((DEPTH, GLA_DV)),
        "w_branch": nrm((DEPTH, N_BRANCH, BRANCH_DIM, D_MODEL), BRANCH_DIM ** -0.5),
        "w_out": nrm((DEPTH, D_MODEL, D_MODEL), D_MODEL ** -0.5),
        "norm_ffn": gain((DEPTH, D_MODEL)),
        "w_ff1": nrm((DEPTH, D_MODEL, D_FF), D_MODEL ** -0.5),
        "w_ff2": nrm((DEPTH, D_FF, D_MODEL), D_FF ** -0.5),
        "norm_ple": gain((DEPTH, D_MODEL)),
        "w_ple_gate": nrm((DEPTH, D_MODEL, D_MODEL), D_MODEL ** -0.5),
        "w_ple_proj": nrm((DEPTH, PLE_DIM, D_MODEL), PLE_DIM ** -0.5),
        "norm_final": gain((D_MODEL,)),
    }


def reference(x_prompt, x_sample, p_prompt, p_sample, state_rwkv, state_shift, state_ssm,
              state_conv, state_gla, norm_mix, w_in, rw_mu, rw_w0, rw_w2, rw_a0, rw_a2, rw_g2,
              rw_kk, rw_ka, rw_rk, rw_ln_w, rw_ln_b, ssm_conv_w, ssm_conv_b, ssm_dt_bias,
              ssm_a_log, ssm_d, ssm_norm, gla_f_up, gla_f_bias, gla_norm, w_branch, w_out,
              norm_ffn, w_ff1, w_ff2, norm_ple, w_ple_gate, w_ple_proj, norm_final):
    layer_w = (norm_mix, w_in, rw_mu, rw_w0, rw_w2, rw_a0, rw_a2, rw_g2, rw_kk, rw_ka, rw_rk,
               rw_ln_w, rw_ln_b, ssm_conv_w, ssm_conv_b, ssm_dt_bias, ssm_a_log, ssm_d, ssm_norm,
               gla_f_up, gla_f_bias, gla_norm, w_branch, w_out, norm_ffn, w_ff1, w_ff2,
               norm_ple, w_ple_gate, w_ple_proj)

    def trunk(x, ple, s_rwkv, s_shift, s_ssm, s_conv, s_gla):
        per_layer = []
        for i in range(DEPTH):
            x, st = decoder_layer(x, ple[i], s_rwkv[i], s_shift[i], s_ssm[i], s_conv[i], s_gla[i],
                                  *(w[i] for w in layer_w))
            per_layer.append(st)
        new = [jnp.stack([st[j] for st in per_layer]) for j in range(5)]
        return rms_norm(x, norm_final), new

    bp = x_prompt.shape[0]
    z = lambda *shape: jnp.zeros((DEPTH, bp) + shape, jnp.float32)
    y_prompt, (rwkv_p, shift_p, ssm_p, conv_p, gla_p) = trunk(
        x_prompt, p_prompt,
        z(RWKV_HEADS, RWKV_HEAD, RWKV_HEAD), z(RWKV_COLS),
        z(SSM_HEADS, SSM_HEADDIM, SSM_STATE), z(SSM_CONV - 1, CONV_CH),
        z(GLA_HEADS, GLA_DK, GLA_DV))
    y_sample, (rwkv_s, shift_s, ssm_s, conv_s, gla_s) = trunk(
        x_sample, p_sample, state_rwkv, state_shift, state_ssm, state_conv, state_gla)
    return (y_prompt, y_sample, rwkv_p, rwkv_s, shift_p, shift_s, ssm_p, ssm_s,
            conv_p, conv_s, gla_p, gla_s)
```

```python
import functools
import math

import jax
import jax.numpy as jnp
from jax import lax
from jax.experimental import pallas as pl
from jax.experimental.pallas import tpu as pltpu

F32 = jnp.float32
BF16 = jnp.bfloat16

BR = 2048
HD = 64
NH = BR // HD
QW = 256
NQ = BR // QW
CH = 64
RW_LW, RW_LA, RW_LG = 96, 96, 256
RW_LN_EPS = 64e-5
SSM_G, SSM_N, SSM_CONV = 4, 128, 4
SSM_XBC = BR + 2 * SSM_G * SSM_N
GLA_H, GLA_DK, GLA_DV, GLA_LORA, GLA_TAU = 4, 256, 512, 16, 16.0
GLA_K = GLA_H * GLA_DK
EPS = 1e-6

RW_R, RW_K, RW_V, RW_WL, RW_AL, RW_GL, RW_PAD = 0, 2048, 4096, 6144, 6272, 6400, 6656
RW_COLS = 3 * BR + RW_LW + RW_LA + RW_LG
SS_Z, SS_X, SS_B, SS_C, SS_DT, SS_PAD = 0, 2048, 4096, 4608, 5120, 5376
SS_COLS = BR + SSM_XBC + NH
GL_Q, GL_K, GL_V, GL_G, GL_F, GL_PAD = 0, 1024, 2048, 4096, 6144, 6400
GL_COLS = 2 * GLA_K + 2 * BR + GLA_LORA

V7X_VMEM_BYTES = 64 * 1024 * 1024
VMEM_REQUEST = 56 * 1024 * 1024
HI = lax.Precision.HIGHEST


def _cparams(sem):
    return pltpu.CompilerParams(dimension_semantics=sem, vmem_limit_bytes=VMEM_REQUEST)


def _row_tile(n, cap, align=16):
    best = None
    for t in range(align, min(n, cap) + 1, align):
        if n % t == 0:
            best = t
    return best if best is not None else n


def _col_tile(n, cap=1280):
    best = None
    for t in range(256, min(n, cap) + 1, 256):
        if n % t == 0:
            best = t
    if best is None:
        for t in range(128, min(n, cap) + 1, 128):
            if n % t == 0:
                best = t
    return best if best is not None else n


def _bdot(a, b):
    return jnp.dot(a.astype(BF16), b.astype(BF16), preferred_element_type=F32)


def _bdot_nt(a, b):
    return lax.dot_general(a.astype(BF16), b.astype(BF16), (((1,), (1,)), ((), ())),
                           preferred_element_type=F32)


def _bdot_tn(a, b):
    return lax.dot_general(a.astype(BF16), b.astype(BF16), (((0,), (0,)), ((), ())),
                           preferred_element_type=F32)


def _hdot(a, b):
    return jnp.dot(a, b, precision=HI, preferred_element_type=F32)


def _hdot_tn(a, b):
    return lax.dot_general(a, b, (((0,), (0,)), ((), ())), precision=HI,
                           preferred_element_type=F32)


def _sigmoid(x):
    return 1.0 / (1.0 + jnp.exp(-x))


def _silu(x):
    return x * _sigmoid(x)


def _softplus(x):
    return jnp.maximum(x, 0.0) + jnp.log(1.0 + jnp.exp(-jnp.abs(x)))


def _log_sigmoid(x):
    return -_softplus(-x)


def _tri_incl(n):
    r = lax.broadcasted_iota(jnp.int32, (n, n), 0)
    c = lax.broadcasted_iota(jnp.int32, (n, n), 1)
    return (c <= r).astype(F32)


def _rms_kernel(x_ref, g_ref, o_ref):
    x = x_ref[...]
    ms = jnp.mean(x * x, axis=-1, keepdims=True)
    o_ref[...] = (x * lax.rsqrt(ms + EPS) * g_ref[...]).astype(o_ref.dtype)


def rms_norm(x, g, out_dtype):
    t, d = x.shape
    tm = _row_tile(t, 520)
    return pl.pallas_call(
        _rms_kernel,
        grid=(t // tm,),
        in_specs=[pl.BlockSpec((tm, d), lambda i: (i, 0)),
                  pl.BlockSpec((1, d), lambda i: (0, 0))],
        out_specs=pl.BlockSpec((tm, d), lambda i: (i, 0)),
        out_shape=jax.ShapeDtypeStruct((t, d), out_dtype),
        compiler_params=_cparams(("parallel",)),
        name="rms_norm",
    )(x, g.reshape(1, d))


def _proj_kernel(x_ref, w_ref, o_ref):
    o_ref[...] = jnp.dot(x_ref[...], w_ref[...], preferred_element_type=F32).astype(o_ref.dtype)


def project(x, w, out_dtype=F32):
    t, k = x.shape
    n = w.shape[1]
    tm, tn = _row_tile(t, 1040), _col_tile(n)
    return pl.pallas_call(
        _proj_kernel,
        grid=(t // tm, n // tn),
        in_specs=[pl.BlockSpec((tm, k), lambda i, j: (i, 0)),
                  pl.BlockSpec((k, tn), lambda i, j: (0, j))],
        out_specs=pl.BlockSpec((tm, tn), lambda i, j: (i, j)),
        out_shape=jax.ShapeDtypeStruct((t, n), out_dtype),
        compiler_params=_cparams(("parallel", "arbitrary")),
        name="project",
    )(x, w)


def _merge_kernel(o1, o2, o3, w1, w2, w3, g1, g2, g3, out):
    acc = _sigmoid(g1[...]) * jnp.dot(o1[...], w1[...], preferred_element_type=F32)
    acc += _sigmoid(g2[...]) * jnp.dot(o2[...], w2[...], preferred_element_type=F32)
    acc += _sigmoid(g3[...]) * jnp.dot(o3[...], w3[...], preferred_element_type=F32)
    out[...] = acc.astype(out.dtype)


def merge_branches(ys, w_branch, u_gate):
    t, kb = ys[0].shape
    d = w_branch.shape[2]
    tm, tn = _row_tile(t, 520), _col_tile(d, 512)
    nb = d // tn
    y_spec = pl.BlockSpec((tm, kb), lambda i, j: (i, 0))
    w_specs = [pl.BlockSpec((None, kb, tn), functools.partial(lambda i, j, b: (b, 0, j), b=b))
               for b in range(3)]
    g_specs = [pl.BlockSpec((tm, tn), functools.partial(lambda i, j, b: (i, b * nb + j), b=b))
               for b in range(3)]
    return pl.pallas_call(
        _merge_kernel,
        grid=(t // tm, nb),
        in_specs=[y_spec, y_spec, y_spec] + w_specs + g_specs,
        out_specs=pl.BlockSpec((tm, tn), lambda i, j: (i, j)),
        out_shape=jax.ShapeDtypeStruct((t, d), BF16),
        compiler_params=_cparams(("parallel", "arbitrary")),
        name="merge_branches",
    )(ys[0], ys[1], ys[2], w_branch, w_branch, w_branch, u_gate, u_gate, u_gate)


def _resid_kernel(a_ref, w_ref, x_ref, o_ref):
    o_ref[...] = x_ref[...] + jnp.dot(a_ref[...], w_ref[...], preferred_element_type=F32)


def residual_matmul(a, w, x):
    t, k = a.shape
    n = w.shape[1]
    tm, tn = _row_tile(t, 1040), _col_tile(n, 512)
    return pl.pallas_call(
        _resid_kernel,
        grid=(t // tm, n // tn),
        in_specs=[pl.BlockSpec((tm, k), lambda i, j: (i, 0)),
                  pl.BlockSpec((k, tn), lambda i, j: (0, j)),
                  pl.BlockSpec((tm, tn), lambda i, j: (i, j))],
        out_specs=pl.BlockSpec((tm, tn), lambda i, j: (i, j)),
        out_shape=jax.ShapeDtypeStruct((t, n), F32),
        compiler_params=_cparams(("parallel", "arbitrary")),
        name="residual_matmul",
    )(a, w, x)


def _ff1_kernel(a_ref, w_ref, o_ref):
    h = jnp.maximum(jnp.dot(a_ref[...], w_ref[...], preferred_element_type=F32), 0.0)
    o_ref[...] = (h * h).astype(o_ref.dtype)


def ffn_up(a, w):
    t, k = a.shape
    n = w.shape[1]
    tm, tn = _row_tile(t, 1040), _col_tile(n, 1024)
    return pl.pallas_call(
        _ff1_kernel,
        grid=(t // tm, n // tn),
        in_specs=[pl.BlockSpec((tm, k), lambda i, j: (i, 0)),
                  pl.BlockSpec((k, tn), lambda i, j: (0, j))],
        out_specs=pl.BlockSpec((tm, tn), lambda i, j: (i, j)),
        out_shape=jax.ShapeDtypeStruct((t, n), BF16),
        compiler_params=_cparams(("parallel", "arbitrary")),
        name="ffn_up",
    )(a, w)


def _ff2_kernel(a_ref, w_ref, x_ref, o_ref, acc_ref):
    kk = pl.program_id(2)

    @pl.when(kk == 0)
    def _():
        acc_ref[...] = x_ref[...]

    acc_ref[...] += jnp.dot(a_ref[...], w_ref[...], preferred_element_type=F32)

    @pl.when(kk == pl.num_programs(2) - 1)
    def _():
        o_ref[...] = acc_ref[...]


def ffn_down(a, w, x):
    t, k = a.shape
    n = w.shape[1]
    tm, tn, tk = _row_tile(t, 1040), _col_tile(n, 1024), _col_tile(k, 2048)
    return pl.pallas_call(
        _ff2_kernel,
        grid=(t // tm, n // tn, k // tk),
        in_specs=[pl.BlockSpec((tm, tk), lambda i, j, l: (i, l)),
                  pl.BlockSpec((tk, tn), lambda i, j, l: (l, j)),
                  pl.BlockSpec((tm, tn), lambda i, j, l: (i, j))],
        out_specs=pl.BlockSpec((tm, tn), lambda i, j, l: (i, j)),
        out_shape=jax.ShapeDtypeStruct((t, n), F32),
        scratch_shapes=[pltpu.VMEM((tm, tn), F32)],
        compiler_params=_cparams(("parallel", "arbitrary", "arbitrary")),
        name="ffn_down",
    )(a, w, x)


def _ple_kernel(p_ref, wp_ref, h_ref, wg_ref, x_ref, o_ref):
    proj = jnp.dot(p_ref[...], wp_ref[...], preferred_element_type=F32)
    gate = _sigmoid(jnp.dot(h_ref[...], wg_ref[...], preferred_element_type=F32))
    o_ref[...] = x_ref[...] + proj * gate


def ple_update(p, wp, hn, wg, x):
    t, kp = p.shape
    k = hn.shape[1]
    n = wg.shape[1]
    tm, tn = _row_tile(t, 1040), _col_tile(n, 512)
    return pl.pallas_call(
        _ple_kernel,
        grid=(t // tm, n // tn),
        in_specs=[pl.BlockSpec((tm, kp), lambda i, j: (i, 0)),
                  pl.BlockSpec((kp, tn), lambda i, j: (0, j)),
                  pl.BlockSpec((tm, k), lambda i, j: (i, 0)),
                  pl.BlockSpec((k, tn), lambda i, j: (0, j)),
                  pl.BlockSpec((tm, tn), lambda i, j: (i, j))],
        out_specs=pl.BlockSpec((tm, tn), lambda i, j: (i, j)),
        out_shape=jax.ShapeDtypeStruct((t, n), F32),
        compiler_params=_cparams(("parallel", "arbitrary")),
        name="ple_update",
    )(p, wp, hn, wg, x)


def _gla_gates(fl, fup_ref, fb_ref):
    return _log_sigmoid(_bdot(fl, fup_ref[...]) + fb_ref[...]) * (1.0 / GLA_TAU)


def _gla_finish(o, g, nw_ref):
    ms = jnp.mean(o * o, axis=-1, keepdims=True)
    return o * lax.rsqrt(ms + EPS) * nw_ref[...] * _silu(g)


def _gla_prompt_kernel(u_ref, fup_ref, fb_ref, nw_ref, y_ref, s_ref):
    c = pl.program_id(1)

    @pl.when(c == 0)
    def _():
        s_ref[...] = jnp.zeros_like(s_ref)

    lg = _gla_gates(u_ref[:, GL_F:GL_F + 128], fup_ref, fb_ref)
    tri = _tri_incl(CH)
    bcum = _hdot(tri, lg)
    blast = bcum[CH - 1:CH, :]
    e_pos = jnp.exp(bcum)
    e_neg = jnp.exp(-bcum)
    e_end = jnp.exp(blast - bcum)
    for h in range(GLA_H):
        ks = slice(h * GLA_DK, (h + 1) * GLA_DK)
        vs = slice(h * GLA_DV, (h + 1) * GLA_DV)
        q = u_ref[:, GL_Q + h * GLA_DK:GL_Q + (h + 1) * GLA_DK] * (GLA_DK ** -0.5)
        k = u_ref[:, GL_K + h * GLA_DK:GL_K + (h + 1) * GLA_DK]
        v = u_ref[:, GL_V + h * GLA_DV:GL_V + (h + 1) * GLA_DV]
        g = u_ref[:, GL_G + h * GLA_DV:GL_G + (h + 1) * GLA_DV]
        q_in = q * e_pos[:, ks]
        k_in = k * e_neg[:, ks]
        k_end = k * e_end[:, ks]
        att = _bdot_nt(q_in, k_in) * tri
        s_prev = s_ref[h]
        o = _bdot(att, v) + _bdot(q_in, s_prev)
        dec = jnp.exp(_hdot_tn(lg[:, ks], jnp.ones((CH, GLA_DV), F32)))
        s_ref[h] = s_prev * dec + _bdot_tn(k_end, v)
        y_ref[:, vs] = _gla_finish(o, g, nw_ref).astype(y_ref.dtype)


def gla_prompt(u, n_seq, seq_len, f_up, f_bias, norm_w):
    nc = seq_len // CH
    y, s = pl.pallas_call(
        _gla_prompt_kernel,
        grid=(n_seq, nc),
        in_specs=[pl.BlockSpec((CH, GL_PAD), lambda b, c: (b * nc + c, 0)),
                  pl.BlockSpec((128, GLA_K), lambda b, c: (0, 0)),
                  pl.BlockSpec((1, GLA_K), lambda b, c: (0, 0)),
                  pl.BlockSpec((1, GLA_DV), lambda b, c: (0, 0))],
        out_specs=[pl.BlockSpec((CH, BR), lambda b, c: (b * nc + c, 0)),
                   pl.BlockSpec((None, GLA_H, GLA_DK, GLA_DV), lambda b, c: (b, 0, 0, 0))],
        out_shape=[jax.ShapeDtypeStruct((n_seq * seq_len, BR), BF16),
                   jax.ShapeDtypeStruct((n_seq, GLA_H, GLA_DK, GLA_DV), F32)],
        compiler_params=_cparams(("parallel", "arbitrary")),
        name="gla_prompt",
    )(u, f_up, f_bias, norm_w)
    return y, s


def _col_of_row(row, eye):
    return jnp.sum(eye * row, axis=-1, keepdims=True)


def _gla_sample_kernel(q_ref, k_ref, v_ref, g_ref, fl_ref, s_ref, fup_ref, fb_ref, nw_ref, y_ref, so_ref):
    nb = q_ref.shape[0]
    alpha = jnp.exp(_gla_gates(fl_ref[...], fup_ref, fb_ref))
    r = lax.broadcasted_iota(jnp.int32, (GLA_DK, GLA_DK), 0)
    c = lax.broadcasted_iota(jnp.int32, (GLA_DK, GLA_DK), 1)
    eye = (r == c).astype(F32)
    for b in range(nb):
        q_col = _col_of_row(q_ref[b:b + 1, :] * (GLA_DK ** -0.5), eye)
        k_col = _col_of_row(k_ref[b:b + 1, :], eye)
        a_col = _col_of_row(alpha[b:b + 1, :], eye)
        s_new = s_ref[b] * a_col + k_col * v_ref[b:b + 1, :]
        so_ref[b] = s_new
        o = jnp.sum(q_col * s_new, axis=0, keepdims=True)
        y_ref[b:b + 1, :] = _gla_finish(o, g_ref[b:b + 1, :], nw_ref)


def gla_sample(u, row0, n_tok, state, layer, f_up, f_bias, norm_w):
    nb = 8
    r0 = row0 // nb
    col = lambda base, w: (lambda i, h: (r0 + i, base // w + h))
    y, s = pl.pallas_call(
        _gla_sample_kernel,
        grid=(n_tok // nb, GLA_H),
        in_specs=[pl.BlockSpec((nb, GLA_DK), col(GL_Q, GLA_DK)),
                  pl.BlockSpec((nb, GLA_DK), col(GL_K, GLA_DK)),
                  pl.BlockSpec((nb, GLA_DV), col(GL_V, GLA_DV)),
                  pl.BlockSpec((nb, GLA_DV), col(GL_G, GLA_DV)),
                  pl.BlockSpec((nb, 128), lambda i, h: (r0 + i, GL_F // 128)),
                  pl.BlockSpec((None, nb, None, GLA_DK, GLA_DV), lambda i, h: (layer, i, h, 0, 0)),
                  pl.BlockSpec((128, GLA_DK), lambda i, h: (0, h)),
                  pl.BlockSpec((1, GLA_DK), lambda i, h: (0, h)),
                  pl.BlockSpec((1, GLA_DV), lambda i, h: (0, 0))],
        out_specs=[pl.BlockSpec((nb, GLA_DV), lambda i, h: (i, h)),
                   pl.BlockSpec((nb, None, GLA_DK, GLA_DV), lambda i, h: (i, h, 0, 0))],
        out_shape=[jax.ShapeDtypeStruct((n_tok, BR), F32),
                   jax.ShapeDtypeStruct((n_tok, GLA_H, GLA_DK, GLA_DV), F32)],
        compiler_params=_cparams(("parallel", "parallel")),
        name="gla_sample",
    )(u, u, u, u, u, state, f_up, f_bias, norm_w)
    return y, s


def _head_expand_matrix():
    r = lax.broadcasted_iota(jnp.int32, (128, BR), 0)
    c = lax.broadcasted_iota(jnp.int32, (128, BR), 1)
    return (r == c // HD).astype(F32)


def _stack4(x, mask4):
    return jnp.concatenate([x, x, x, x], axis=0) * mask4


def _quad_masks():
    r = lax.broadcasted_iota(jnp.int32, (4 * CH, QW), 0)
    c = lax.broadcasted_iota(jnp.int32, (4 * CH, QW), 1)
    return (r // CH == c // HD).astype(F32)


def _ssm_prompt_kernel(u_ref, cw_ref, cb_ref, dtb_ref, a_ref, d_ref, nw_ref, y_ref, h_ref,
                       carry_ref, x_s, dtl_s, cum_s, lal_s, yo_s):
    c = pl.program_id(1)

    @pl.when(c == 0)
    def _():
        h_ref[...] = jnp.zeros_like(h_ref)
        carry_ref[...] = jnp.zeros_like(carry_ref)

    pre = u_ref[:, SS_X:SS_X + SSM_XBC]
    prev = carry_ref[...]
    row8 = lax.broadcasted_iota(jnp.int32, (8, SSM_XBC), 0)
    acc = cb_ref[...] + pre * cw_ref[SSM_CONV - 1:SSM_CONV, :]
    for j in range(1, SSM_CONV):
        sh = pltpu.roll(pre, j, axis=0)
        top = jnp.where(row8 < j, pltpu.roll(prev, j, axis=0), sh[0:8, :])
        sh = jnp.concatenate([top, sh[8:, :]], axis=0)
        acc = acc + sh * cw_ref[SSM_CONV - 1 - j:SSM_CONV - j, :]
    carry_ref[...] = pre[CH - 8:CH, :]
    xbc = _silu(acc)
    x_s[...] = xbc[:, 0:BR]

    expand = _head_expand_matrix()
    tri = _tri_incl(CH)
    dt = _softplus(u_ref[:, SS_DT:SS_DT + 128] + dtb_ref[...])
    la = dt * a_ref[...]
    dtl_s[...] = _hdot(dt, expand)
    lal = _hdot(la, expand)
    lal_s[...] = lal
    cum_s[...] = _hdot(tri, lal)

    mask4 = _quad_masks()
    t_i = lax.broadcasted_iota(jnp.int32, (CH, 4 * CH), 0)
    s_i = lax.broadcasted_iota(jnp.int32, (CH, 4 * CH), 1) % CH
    causal = s_i <= t_i
    upto = (t_i <= s_i).astype(F32)
    ones_cn = jnp.ones((CH, SSM_N), F32)
    for q in range(NQ):
        g = q // (NQ // SSM_G)
        ls = slice(q * QW, (q + 1) * QW)
        bm = xbc[:, BR + g * SSM_N:BR + (g + 1) * SSM_N]
        cm = xbc[:, BR + SSM_G * SSM_N + g * SSM_N:BR + SSM_G * SSM_N + (g + 1) * SSM_N]
        cum_q = cum_s[:, ls]
        lal_q = lal_s[:, ls]
        xdt = x_s[:, ls] * dtl_s[:, ls]
        cum_row = jnp.sum(lal_q * upto, axis=0, keepdims=True)
        lmat = jnp.exp(jnp.where(causal, cum_q - cum_row, -1e30))
        cb = _bdot_nt(cm, jnp.concatenate([bm, bm, bm, bm], axis=0))
        y_diag = _bdot(cb * lmat, _stack4(xdt, mask4))
        total = cum_q[CH - 1:CH, :]
        h_prev = h_ref[q * QW:(q + 1) * QW, :]
        y_off = _bdot_nt(cm, h_prev) * jnp.exp(cum_q)
        dec = jnp.exp(_hdot_tn(lal_q, ones_cn))
        h_ref[q * QW:(q + 1) * QW, :] = h_prev * dec + _bdot_tn(xdt * jnp.exp(total - cum_q), bm)
        yo_s[:, ls] = y_diag + y_off + d_ref[:, ls] * x_s[:, ls]

    gw = BR // SSM_G
    for g in range(SSM_G):
        gs = slice(g * gw, (g + 1) * gw)
        yg = yo_s[:, gs] * _silu(u_ref[:, SS_Z + g * gw:SS_Z + (g + 1) * gw])
        ms = jnp.mean(yg * yg, axis=-1, keepdims=True)
        y_ref[:, gs] = (yg * lax.rsqrt(ms + EPS) * nw_ref[:, gs]).astype(y_ref.dtype)


def ssm_prompt(u, n_seq, seq_len, conv_w, conv_b, dt_bias, a_neg, d_lane, norm_w):
    nc = seq_len // CH
    const = lambda shape: pl.BlockSpec(shape, lambda b, c: (0, 0))
    y, h = pl.pallas_call(
        _ssm_prompt_kernel,
        grid=(n_seq, nc),
        in_specs=[pl.BlockSpec((CH, SS_PAD), lambda b, c: (b * nc + c, 0)),
                  const((SSM_CONV, SSM_XBC)), const((1, SSM_XBC)), const((1, 128)), const((1, 128)),
                  const((1, BR)), const((1, BR))],
        out_specs=[pl.BlockSpec((CH, BR), lambda b, c: (b * nc + c, 0)),
                   pl.BlockSpec((None, BR, SSM_N), lambda b, c: (b, 0, 0))],
        out_shape=[jax.ShapeDtypeStruct((n_seq * seq_len, BR), BF16),
                   jax.ShapeDtypeStruct((n_seq, BR, SSM_N), F32)],
        scratch_shapes=[pltpu.VMEM((8, SSM_XBC), F32)] + [pltpu.VMEM((CH, BR), F32)] * 5,
        compiler_params=_cparams(("parallel", "arbitrary")),
        name="ssm_prompt",
    )(u, conv_w, conv_b, dt_bias, a_neg, d_lane, norm_w)
    return y, h


def _ssm_sample_kernel(x_ref, b_ref, c_ref, dt_ref, z_ref, sx_ref, sb_ref, sc_ref,
                       cwx_ref, cwb_ref, cwc_ref, cbx_ref, cbb_ref, cbc_ref,
                       dtb_ref, a_ref, d_ref, nw_ref, h_ref, y_ref, ho_ref):
    nb = x_ref.shape[0]
    gw = x_ref.shape[1]

    def conv(cur_ref, st_ref, w_ref, bias_ref):
        acc = bias_ref[...] + cur_ref[...] * w_ref[SSM_CONV - 1:SSM_CONV, :]
        for j in range(SSM_CONV - 1):
            acc = acc + st_ref[j] * w_ref[j:j + 1, :]
        return _silu(acc)

    xs = conv(x_ref, sx_ref, cwx_ref, cbx_ref)
    bm = conv(b_ref, sb_ref, cwb_ref, cbb_ref)
    cm = conv(c_ref, sc_ref, cwc_ref, cbc_ref)
    g = pl.program_id(1)
    r = lax.broadcasted_iota(jnp.int32, (128, gw), 0)
    col = lax.broadcasted_iota(jnp.int32, (128, gw), 1)
    expand = (r == col // HD + g * (gw // HD)).astype(F32)
    dt = _softplus(dt_ref[...] + dtb_ref[...])
    dtl = _hdot(dt, expand)
    decl = jnp.exp(_hdot(dt * a_ref[...], expand))
    xdt = xs * dtl
    rr = lax.broadcasted_iota(jnp.int32, (gw, gw), 0)
    cc = lax.broadcasted_iota(jnp.int32, (gw, gw), 1)
    eye = (rr == cc).astype(F32)
    for b in range(nb):
        dec_col = _col_of_row(decl[b:b + 1, :], eye)
        xdt_col = _col_of_row(xdt[b:b + 1, :], eye)
        h_new = h_ref[b] * dec_col + xdt_col * bm[b:b + 1, :]
        ho_ref[b] = h_new
        y_col = jnp.sum(h_new * cm[b:b + 1, :], axis=-1, keepdims=True)
        y_row = jnp.sum(eye * y_col, axis=0, keepdims=True)
        y_row = (y_row + d_ref[...] * xs[b:b + 1, :]) * _silu(z_ref[b:b + 1, :])
        ms = jnp.mean(y_row * y_row, axis=-1, keepdims=True)
        y_ref[b:b + 1, :] = y_row * lax.rsqrt(ms + EPS) * nw_ref[...]


def ssm_sample(u, row0, n_tok, state, conv_state_t, layer, conv_w, conv_b, dt_bias, a_neg, d_lane, norm_w):
    nb = 8
    r0 = row0 // nb
    gw = BR // SSM_G
    xb, bb, cb = 0, BR // SSM_N, (BR + SSM_G * SSM_N) // SSM_N
    ucol = lambda base, w: (lambda i, g: (r0 + i, base // w + g))
    scol = lambda base: (lambda i, g: (0, i, base + g))
    wcol = lambda base: (lambda i, g: (0, base + g))
    y, h = pl.pallas_call(
        _ssm_sample_kernel,
        grid=(n_tok // nb, SSM_G),
        in_specs=[pl.BlockSpec((nb, gw), ucol(SS_X, gw)),
                  pl.BlockSpec((nb, SSM_N), ucol(SS_B, SSM_N)),
                  pl.BlockSpec((nb, SSM_N), ucol(SS_C, SSM_N)),
                  pl.BlockSpec((nb, 128), lambda i, g: (r0 + i, SS_DT // 128)),
                  pl.BlockSpec((nb, gw), ucol(SS_Z, gw)),
                  pl.BlockSpec((SSM_CONV - 1, nb, gw), lambda i, g: (0, i, g)),
                  pl.BlockSpec((SSM_CONV - 1, nb, SSM_N), scol(bb)),
                  pl.BlockSpec((SSM_CONV - 1, nb, SSM_N), scol(cb)),
                  pl.BlockSpec((SSM_CONV, gw), lambda i, g: (0, g)),
                  pl.BlockSpec((SSM_CONV, SSM_N), wcol(bb)),
                  pl.BlockSpec((SSM_CONV, SSM_N), wcol(cb)),
                  pl.BlockSpec((1, gw), lambda i, g: (0, g)),
                  pl.BlockSpec((1, SSM_N), wcol(bb)),
                  pl.BlockSpec((1, SSM_N), wcol(cb)),
                  pl.BlockSpec((1, 128), lambda i, g: (0, 0)),
                  pl.BlockSpec((1, 128), lambda i, g: (0, 0)),
                  pl.BlockSpec((1, gw), lambda i, g: (0, g)),
                  pl.BlockSpec((1, gw), lambda i, g: (0, g)),
                  pl.BlockSpec((None, nb, None, gw, SSM_N), lambda i, g: (layer, i, g, 0, 0))],
        out_specs=[pl.BlockSpec((nb, gw), lambda i, g: (i, g)),
                   pl.BlockSpec((nb, None, gw, SSM_N), lambda i, g: (i, g, 0, 0))],
        out_shape=[jax.ShapeDtypeStruct((n_tok, BR), F32),
                   jax.ShapeDtypeStruct((n_tok, SSM_G, gw, SSM_N), F32)],
        compiler_params=_cparams(("parallel", "parallel")),
        name="ssm_sample",
    )(u, u, u, u, u, conv_state_t, conv_state_t, conv_state_t, conv_w, conv_w, conv_w,
      conv_b, conv_b, conv_b, dt_bias, a_neg, d_lane, norm_w, state)
    return y, h


def _rw_mix_inputs(xs_of, w0_ref, w2_ref, a0_ref, a2_ref, g2_ref):
    r = xs_of(RW_R, BR)
    k = xs_of(RW_K, BR)
    v = xs_of(RW_V, BR)
    w = -_softplus(-(w0_ref[...] + _bdot(jnp.tanh(xs_of(RW_WL, 128)), w2_ref[...]))) - 0.5
    log_w = -jnp.exp(w)
    a = _sigmoid(a0_ref[...] + _bdot(xs_of(RW_AL, 128), a2_ref[...]))
    g = _bdot(_sigmoid(xs_of(RW_GL, RW_LG)), g2_ref[...])
    return r, k, v, log_w, a, g


def _seg_sum(x, ones_bd):
    hi = x.astype(BF16)
    lo = (x - hi.astype(F32)).astype(BF16)
    return (jnp.dot(hi, ones_bd, preferred_element_type=F32)
            + jnp.dot(lo, ones_bd, preferred_element_type=F32))


def _rw_prompt_kernel(u_ref, mu_ref, w0_ref, w2_ref, a0_ref, a2_ref, g2_ref, kk_ref, ka_ref, rk_ref,
                      lnw_ref, lnb_ref, y_ref, s_ref, carry_ref, r_s, k_s, v_s, lw_s, a_s, g_s):
    c = pl.program_id(1)

    @pl.when(c == 0)
    def _():
        s_ref[...] = jnp.zeros_like(s_ref)
        carry_ref[...] = jnp.zeros_like(carry_ref)

    row8 = lax.broadcasted_iota(jnp.int32, (8, 1), 0)

    def xs_of(c0, width):
        cur = u_ref[:, c0:c0 + width]
        sh = pltpu.roll(cur, 1, axis=0)
        top = jnp.where(row8 < 1, pltpu.roll(carry_ref[:, c0:c0 + width], 1, axis=0), sh[0:8, :])
        prev = jnp.concatenate([top, sh[8:, :]], axis=0)
        return cur + (prev - cur) * mu_ref[:, c0:c0 + width]

    r, k, v, log_w, a, g = _rw_mix_inputs(xs_of, w0_ref, w2_ref, a0_ref, a2_ref, g2_ref)
    r_s[...] = r
    k_s[...] = k
    v_s[...] = v
    lw_s[...] = log_w
    a_s[...] = a
    g_s[...] = g
    carry_ref[...] = u_ref[CH - 8:CH, :]

    tri = _tri_incl(CH)
    mask4 = _quad_masks()
    rb = lax.broadcasted_iota(jnp.int32, (QW, QW), 0)
    cb = lax.broadcasted_iota(jnp.int32, (QW, QW), 1)
    bd = (rb // HD == cb // HD)
    bdf = bd.astype(F32)
    ones_bd = bd.astype(BF16)
    t_i = lax.broadcasted_iota(jnp.int32, (CH, 4 * CH), 0)
    s_i = lax.broadcasted_iota(jnp.int32, (CH, 4 * CH), 1) % CH
    strict = (s_i < t_i).astype(F32)
    incl = (s_i <= t_i).astype(F32)
    ident = (s_i == t_i).astype(F32)

    def blockdiag(x):
        return jnp.concatenate([x, x, x, x], axis=0) * bdf

    for q in range(NQ):
        ls = slice(q * QW, (q + 1) * QW)
        r_q, k_q, v_q, lw_q, a_q, g_q = r_s[:, ls], k_s[:, ls], v_s[:, ls], lw_s[:, ls], a_s[:, ls], g_s[:, ls]
        kk = k_q * kk_ref[:, ls]
        kk = kk / jnp.maximum(jnp.sqrt(_seg_sum(kk * kk, ones_bd)), 1e-12)
        kmod = k_q * (1.0 + (a_q - 1.0) * ka_ref[:, ls])
        av = -kk
        bv = kk * a_q

        cum = _hdot(tri, lw_q)
        total = cum[CH - 1:CH, :]
        e_neg = jnp.exp(-cum)
        e_end = jnp.exp(total - cum)
        at = av * jnp.exp(cum - lw_q)
        rt = r_q * jnp.exp(cum)
        bt = bv * e_neg
        kt = kmod * e_neg
        bbar = bv * e_end
        kbar = kmod * e_end

        gram = _bdot_nt(jnp.concatenate([at, rt], axis=0),
                        jnp.concatenate([_stack4(bt, mask4), _stack4(kt, mask4)], axis=0))
        a_ab = gram[0:CH, 0:4 * CH] * strict
        a_ak = gram[0:CH, 4 * CH:8 * CH] * strict
        m_rb = gram[CH:2 * CH, 0:4 * CH] * incl
        m_rk = gram[CH:2 * CH, 4 * CH:8 * CH] * incl

        tinv = ident + a_ab
        p = a_ab
        for _ in range(5):
            p = _bdot(p, blockdiag(p))
            tinv = tinv + _bdot(tinv, blockdiag(p))

        a_hat = _bdot(tinv, _stack4(at, mask4))
        w_v = _bdot(tinv, _stack4(_bdot(a_ak, _stack4(v_q, mask4)), mask4))
        q_hat = rt + _bdot(m_rb, _stack4(a_hat, mask4))
        o_loc = _bdot(jnp.concatenate([m_rb, m_rk], axis=1),
                      jnp.concatenate([_stack4(w_v, mask4), _stack4(v_q, mask4)], axis=0))
        s_prev = s_ref[q]
        o = _bdot_nt(q_hat, s_prev) + o_loc
        g_corr = _bdot_tn(a_hat, bbar) * bdf
        h_new = _bdot_tn(jnp.concatenate([w_v, v_q], axis=0), jnp.concatenate([bbar, kbar], axis=0)) * bdf
        s_ref[q] = s_prev * jnp.exp(total) + _bdot(s_prev, g_corr) + h_new

        mean = _seg_sum(o, ones_bd) * (1.0 / HD)
        d = o - mean
        var = _seg_sum(d * d, ones_bd) * (1.0 / HD)
        on = d * lax.rsqrt(var + RW_LN_EPS) * lnw_ref[:, ls] + lnb_ref[:, ls]
        bonus = _seg_sum(r_q * kmod * rk_ref[:, ls], ones_bd) * v_q
        y_ref[:, ls] = ((on + bonus) * g_q).astype(y_ref.dtype)


def rwkv_prompt(u, n_seq, seq_len, mu, w0, w2, a0, a2, g2, k_k, k_a, r_k, ln_w, ln_b):
    nc = seq_len // CH
    const = lambda shape: pl.BlockSpec(shape, lambda b, c: (0, 0))
    y, s = pl.pallas_call(
        _rw_prompt_kernel,
        grid=(n_seq, nc),
        in_specs=[pl.BlockSpec((CH, RW_PAD), lambda b, c: (b * nc + c, 0)),
                  const((1, RW_PAD)), const((1, BR)), const((128, BR)), const((1, BR)), const((128, BR)),
                  const((RW_LG, BR)), const((1, BR)), const((1, BR)), const((1, BR)), const((1, BR)),
                  const((1, BR))],
        out_specs=[pl.BlockSpec((CH, BR), lambda b, c: (b * nc + c, 0)),
                   pl.BlockSpec((None, NQ, QW, QW), lambda b, c: (b, 0, 0, 0))],
        out_shape=[jax.ShapeDtypeStruct((n_seq * seq_len, BR), BF16),
                   jax.ShapeDtypeStruct((n_seq, NQ, QW, QW), F32)],
        scratch_shapes=[pltpu.VMEM((8, RW_PAD), F32)] + [pltpu.VMEM((CH, BR), F32)] * 6,
        compiler_params=_cparams(("parallel", "arbitrary")),
        name="rwkv_prompt",
    )(u, mu, w0, w2, a0, a2, g2, k_k, k_a, r_k, ln_w, ln_b)
    return y, s


def _rw_sample_pre_kernel(u_ref, sh_ref, mu_ref, w0_ref, w2_ref, a0_ref, a2_ref, g2_ref,
                          r_o, k_o, v_o, w_o, a_o, g_o):
    def xs_of(c0, width):
        cur = u_ref[:, c0:c0 + width]
        return cur + (sh_ref[:, c0:c0 + width] - cur) * mu_ref[:, c0:c0 + width]

    r, k, v, log_w, a, g = _rw_mix_inputs(xs_of, w0_ref, w2_ref, a0_ref, a2_ref, g2_ref)
    r_o[...] = r
    k_o[...] = k
    v_o[...] = v
    w_o[...] = jnp.exp(log_w)
    a_o[...] = a
    g_o[...] = g


def _rw_sample_step_kernel(r_ref, k_ref, v_ref, w_ref, a_ref, g_ref, kk_ref, ka_ref, rk_ref,
                           lnw_ref, lnb_ref, s_ref, y_ref, so_ref):
    rr = lax.broadcasted_iota(jnp.int32, (HD, HD), 0)
    cc = lax.broadcasted_iota(jnp.int32, (HD, HD), 1)
    eye = (rr == cc).astype(F32)[None]

    def body(b, carry):
        r, k, v, w, a, g = r_ref[b], k_ref[b], v_ref[b], w_ref[b], a_ref[b], g_ref[b]
        kk = k * kk_ref[...]
        kk = kk / jnp.maximum(jnp.sqrt(jnp.sum(kk * kk, axis=-1, keepdims=True)), 1e-12)
        kmod = k * (1.0 + (a - 1.0) * ka_ref[...])
        s = s_ref[b]
        sa = jnp.sum(s * (-kk), axis=-1, keepdims=True)
        v_col = jnp.sum(eye * v, axis=-1, keepdims=True)
        s_new = s * w + sa * (kk * a) + v_col * kmod
        so_ref[b] = s_new
        o_col = jnp.sum(s_new * r, axis=-1, keepdims=True)
        o = jnp.sum(eye * o_col, axis=1, keepdims=True)
        mean = jnp.mean(o, axis=-1, keepdims=True)
        d = o - mean
        var = jnp.mean(d * d, axis=-1, keepdims=True)
        on = d * lax.rsqrt(var + RW_LN_EPS) * lnw_ref[...] + lnb_ref[...]
        bonus = jnp.sum(r * kmod * rk_ref[...], axis=-1, keepdims=True) * v
        y_ref[b] = (on + bonus) * g
        return carry

    lax.fori_loop(0, r_ref.shape[0], body, 0)


def rwkv_sample(u, row0, n_tok, state, shift_prev, layer, mu, w0, w2, a0, a2, g2, k_k, k_a, r_k, ln_w, ln_b):
    const = lambda shape: pl.BlockSpec(shape, lambda i: (0, 0))
    vec = jax.ShapeDtypeStruct((n_tok, BR), F32)
    outs = pl.pallas_call(
        _rw_sample_pre_kernel,
        grid=(1,),
        in_specs=[pl.BlockSpec((n_tok, RW_PAD), lambda i: (row0 // n_tok, 0)),
                  pl.BlockSpec((None, n_tok, RW_PAD), lambda i: (layer, 0, 0)),
                  const((1, RW_PAD)), const((1, BR)), const((128, BR)), const((1, BR)), const((128, BR)),
                  const((RW_LG, BR))],
        out_specs=[pl.BlockSpec((n_tok, BR), lambda i: (0, 0))] * 6,
        out_shape=[vec] * 6,
        compiler_params=_cparams(("arbitrary",)),
        name="rwkv_sample_pre",
    )(u, shift_prev, mu, w0, w2, a0, a2, g2)
    heads = lambda t: t.reshape(-1, NH, 1, HD)
    nb = 8
    vspec = pl.BlockSpec((nb, NH, 1, HD), lambda i: (i, 0, 0, 0))
    pspec = pl.BlockSpec((NH, 1, HD), lambda i: (0, 0, 0))
    y, s = pl.pallas_call(
        _rw_sample_step_kernel,
        grid=(n_tok // nb,),
        in_specs=[vspec] * 6 + [pspec] * 5
        + [pl.BlockSpec((None, nb, NH, HD, HD), lambda i: (layer, i, 0, 0, 0))],
        out_specs=[vspec, pl.BlockSpec((nb, NH, HD, HD), lambda i: (i, 0, 0, 0))],
        out_shape=[jax.ShapeDtypeStruct((n_tok, NH, 1, HD), F32),
                   jax.ShapeDtypeStruct((n_tok, NH, HD, HD), F32)],
        compiler_params=_cparams(("parallel",)),
        name="rwkv_sample_step",
    )(*[heads(t) for t in outs], *[t.reshape(NH, 1, HD) for t in (k_k, k_a, r_k, ln_w, ln_b)], state)
    return y.reshape(n_tok, BR), s


def _pad_last(t, width):
    return jnp.pad(t, [(0, 0)] * (t.ndim - 1) + [(0, width - t.shape[-1])])


def _rw_to_padded(t):
    r_k_v = t[..., :3 * BR]
    wl = t[..., 3 * BR:3 * BR + RW_LW]
    al = t[..., 3 * BR + RW_LW:3 * BR + RW_LW + RW_LA]
    gl = t[..., 3 * BR + RW_LW + RW_LA:]
    return jnp.concatenate([r_k_v, _pad_last(wl, 128), _pad_last(al, 128), gl], axis=-1)


def _rw_from_padded(t):
    return jnp.concatenate([t[..., :RW_WL], t[..., RW_WL:RW_WL + RW_LW], t[..., RW_AL:RW_AL + RW_LA],
                            t[..., RW_GL:RW_GL + RW_LG]], axis=-1)


def _split_w_in(w, d_model):
    o1 = RW_COLS
    o2 = o1 + SS_COLS
    o3 = o2 + GL_COLS
    w = w.astype(BF16)
    w_rw = _rw_to_padded(w[:, :o1])
    w_ss = _pad_last(w[:, o1:o2], SS_PAD)
    w_gl = _pad_last(w[:, o2:o3], GL_PAD)
    return w_rw, w_ss, w_gl, w[:, o3:]


def _row(v):
    return v.reshape(1, -1).astype(F32)


def kernel(x_prompt, x_sample, p_prompt, p_sample, state_rwkv, state_shift, state_ssm, state_conv, state_gla, norm_mix, w_in, rw_mu, rw_w0, rw_w2, rw_a0, rw_a2, rw_g2, rw_kk, rw_ka, rw_rk, rw_ln_w, rw_ln_b, ssm_conv_w, ssm_conv_b, ssm_dt_bias, ssm_a_log, ssm_d, ssm_norm, gla_f_up, gla_f_bias, gla_norm, w_branch, w_out, norm_ffn, w_ff1, w_ff2, norm_ple, w_ple_gate, w_ple_proj, norm_final):
    depth = w_in.shape[0]
    bp, seq, d = x_prompt.shape
    ns = x_sample.shape[0]
    assert x_sample.shape[1] == 1, "the sample group is decoded one token at a time"
    assert seq % CH == 0 and seq >= 8
    tp = bp * seq
    assert tp % ns == 0 and ns % 8 == 0

    x = jnp.concatenate([x_prompt.reshape(tp, d), x_sample.reshape(ns, d)], axis=0)
    ple = jnp.concatenate([p_prompt.reshape(depth, tp, -1), p_sample.reshape(depth, ns, -1)], axis=1).astype(BF16)
    shift_pad = _rw_to_padded(state_shift)
    ssm_state = state_ssm.reshape(depth, ns, SSM_G, BR // SSM_G, SSM_N)
    conv_t = jnp.transpose(state_conv, (0, 2, 1, 3))

    outs = {k: [] for k in ("rw_p", "rw_s", "sh_p", "sh_s", "ss_p", "ss_s", "cv_p", "cv_s", "gl_p", "gl_s")}
    h = rms_norm(x, norm_mix[0], BF16)
    for i in range(depth):
        w_rw, w_ss, w_gl, w_gate = _split_w_in(w_in[i], d)
        u_rw = project(h, w_rw)
        u_ss = project(h, w_ss)
        u_gl = project(h, w_gl)
        u_gate = project(h, w_gate)

        rw_par = (_rw_to_padded(_row(rw_mu[i])), _row(rw_w0[i]),
                  jnp.pad(rw_w2[i], ((0, 128 - RW_LW), (0, 0))).astype(BF16), _row(rw_a0[i]),
                  jnp.pad(rw_a2[i], ((0, 128 - RW_LA), (0, 0))).astype(BF16), rw_g2[i].astype(BF16),
                  _row(rw_kk[i]), _row(rw_ka[i]), _row(rw_rk[i]), _row(rw_ln_w[i]), _row(rw_ln_b[i]))
        y_rw_p, s_bd = rwkv_prompt(u_rw, bp, seq, *rw_par)
        y_rw_s, s_rw = rwkv_sample(u_rw, tp, ns, state_rwkv, shift_pad, i, *rw_par)
        sb = s_bd.reshape(bp, NQ, 4, HD, 4, HD)
        outs["rw_p"].append(jnp.stack([sb[:, :, j, :, j, :] for j in range(4)], axis=2).reshape(bp, NH, HD, HD))
        outs["rw_s"].append(s_rw)
        outs["sh_p"].append(_rw_from_padded(u_rw[seq - 1:tp:seq]))
        outs["sh_s"].append(_rw_from_padded(u_rw[tp:]))

        pad128 = lambda v: jnp.pad(v.astype(F32), (0, 128 - v.shape[0])).reshape(1, 128)
        ss_par = (ssm_conv_w[i], _row(ssm_conv_b[i]), pad128(ssm_dt_bias[i]),
                  pad128(-jnp.exp(ssm_a_log[i].astype(F32))), _row(jnp.repeat(ssm_d[i], HD)), _row(ssm_norm[i]))
        y_ss_p, h_p = ssm_prompt(u_ss, bp, seq, *ss_par)
        y_ss_s, h_s = ssm_sample(u_ss, tp, ns, ssm_state, conv_t[i], i, *ss_par)
        outs["ss_p"].append(h_p.reshape(bp, NH, HD, SSM_N))
        outs["ss_s"].append(h_s.reshape(ns, NH, HD, SSM_N))
        xbc_p = u_ss[:tp, SS_X:SS_X + SSM_XBC].reshape(bp, seq, SSM_XBC)
        outs["cv_p"].append(xbc_p[:, seq - (SSM_CONV - 1):])
        outs["cv_s"].append(jnp.concatenate([state_conv[i][:, 1:], u_ss[tp:, None, SS_X:SS_X + SSM_XBC]], axis=1))

        gl_par = (jnp.pad(gla_f_up[i], ((0, 128 - GLA_LORA), (0, 0))).astype(BF16), _row(gla_f_bias[i]),
                  _row(gla_norm[i]))
        y_gl_p, g_p = gla_prompt(u_gl, bp, seq, *gl_par)
        y_gl_s, g_s = gla_sample(u_gl, tp, ns, state_gla, i, *gl_par)
        outs["gl_p"].append(g_p)
        outs["gl_s"].append(g_s)

        ys = [jnp.concatenate([yp, ysm.astype(BF16)], axis=0)
              for yp, ysm in ((y_rw_p, y_rw_s), (y_ss_p, y_ss_s), (y_gl_p, y_gl_s))]
        merged = merge_branches(ys, w_branch[i].astype(BF16), u_gate)
        x = residual_matmul(merged, w_out[i].astype(BF16), x)
        hf = rms_norm(x, norm_ffn[i], BF16)
        x = ffn_down(ffn_up(hf, w_ff1[i].astype(BF16)), w_ff2[i].astype(BF16), x)
        hn = rms_norm(x, norm_ple[i], BF16)
        x = ple_update(ple[i], w_ple_proj[i].astype(BF16), hn, w_ple_gate[i].astype(BF16), x)
        if i + 1 < depth:
            h = rms_norm(x, norm_mix[i + 1], BF16)

    y = rms_norm(x, norm_final, F32)
    st = lambda k: jnp.stack(outs[k])
    return (y[:tp].reshape(bp, seq, d), y[tp:].reshape(ns, 1, d), st("rw_p"), st("rw_s"), st("sh_p"), st("sh_s"),
            st("ss_p"), st("ss_s"), st("cv_p"), st("cv_s"), st("gl_p"), st("gl_s"))
```

```python
import functools
import math

import jax
import jax.numpy as jnp
from jax import lax
from jax.experimental import pallas as pl
from jax.experimental.pallas import tpu as pltpu

F32 = jnp.float32
BF16 = jnp.bfloat16

BR = 2048
HD = 64
NH = BR // HD
QW = 256
NQ = BR // QW
CH = 64
RW_LW, RW_LA, RW_LG = 96, 96, 256
RW_LN_EPS = 64e-5
SSM_G, SSM_N, SSM_CONV = 4, 128, 4
SSM_XBC = BR + 2 * SSM_G * SSM_N
GLA_H, GLA_DK, GLA_DV, GLA_LORA, GLA_TAU = 4, 256, 512, 16, 16.0
GLA_K = GLA_H * GLA_DK
EPS = 1e-6

RW_R, RW_K, RW_V, RW_WL, RW_AL, RW_GL, RW_PAD = 0, 2048, 4096, 6144, 6272, 6400, 6656
RW_COLS = 3 * BR + RW_LW + RW_LA + RW_LG
SS_Z, SS_X, SS_B, SS_C, SS_DT, SS_PAD = 0, 2048, 4096, 4608, 5120, 5376
SS_COLS = BR + SSM_XBC + NH
GL_Q, GL_K, GL_V, GL_G, GL_F, GL_PAD = 0, 1024, 2048, 4096, 6144, 6400
GL_COLS = 2 * GLA_K + 2 * BR + GLA_LORA

V7X_VMEM_BYTES = 64 * 1024 * 1024
VMEM_REQUEST = 56 * 1024 * 1024
HI = lax.Precision.HIGHEST


def _cparams(sem):
    return pltpu.CompilerParams(dimension_semantics=sem, vmem_limit_bytes=VMEM_REQUEST)


def _row_tile(n, cap, align=16):
    best = None
    for t in range(align, min(n, cap) + 1, align):
        if n % t == 0:
            best = t
    return best if best is not None else n


def _col_tile(n, cap=1280):
    best = None
    for t in range(256, min(n, cap) + 1, 256):
        if n % t == 0:
            best = t
    if best is None:
        for t in range(128, min(n, cap) + 1, 128):
            if n % t == 0:
                best = t
    return best if best is not None else n


def _bdot(a, b):
    return jnp.dot(a.astype(BF16), b.astype(BF16), preferred_element_type=F32)


def _bdot_nt(a, b):
    return lax.dot_general(a.astype(BF16), b.astype(BF16), (((1,), (1,)), ((), ())),
                           preferred_element_type=F32)


def _bdot_tn(a, b):
    return lax.dot_general(a.astype(BF16), b.astype(BF16), (((0,), (0,)), ((), ())),
                           preferred_element_type=F32)


def _hdot(a, b):
    return jnp.dot(a, b, precision=HI, preferred_element_type=F32)


def _hdot_tn(a, b):
    return lax.dot_general(a, b, (((0,), (0,)), ((), ())), precision=HI,
                           preferred_element_type=F32)


def _sigmoid(x):
    return 1.0 / (1.0 + jnp.exp(-x))


def _silu(x):
    return x * _sigmoid(x)


def _softplus(x):
    return jnp.maximum(x, 0.0) + jnp.log(1.0 + jnp.exp(-jnp.abs(x)))


def _log_sigmoid(x):
    return -_softplus(-x)


def _tri_incl(n):
    r = lax.broadcasted_iota(jnp.int32, (n, n), 0)
    c = lax.broadcasted_iota(jnp.int32, (n, n), 1)
    return (c <= r).astype(F32)


def _rms_kernel(x_ref, g_ref, o_ref):
    x = x_ref[...]
    ms = jnp.mean(x * x, axis=-1, keepdims=True)
    o_ref[...] = (x * lax.rsqrt(ms + EPS) * g_ref[...]).astype(o_ref.dtype)


def rms_norm(x, g, out_dtype):
    t, d = x.shape
    tm = _row_tile(t, 520)
    return pl.pallas_call(
        _rms_kernel,
        grid=(t // tm,),
        in_specs=[pl.BlockSpec((tm, d), lambda i: (i, 0)),
                  pl.BlockSpec((1, d), lambda i: (0, 0))],
        out_specs=pl.BlockSpec((tm, d), lambda i: (i, 0)),
        out_shape=jax.ShapeDtypeStruct((t, d), out_dtype),
        compiler_params=_cparams(("parallel",)),
        name="rms_norm",
    )(x, g.reshape(1, d))


def _proj_kernel(x_ref, w_ref, o_ref):
    o_ref[...] = jnp.dot(x_ref[...], w_ref[...], preferred_element_type=F32).astype(o_ref.dtype)


def project(x, w, layer, out_dtype=F32):
    t, k = x.shape
    n = w.shape[2]
    tm, tn = _row_tile(t, 1040), _col_tile(n)
    return pl.pallas_call(
        _proj_kernel,
        grid=(t // tm, n // tn),
        in_specs=[pl.BlockSpec((tm, k), lambda i, j: (i, 0)),
                  pl.BlockSpec((None, k, tn), lambda i, j: (layer, 0, j))],
        out_specs=pl.BlockSpec((tm, tn), lambda i, j: (i, j)),
        out_shape=jax.ShapeDtypeStruct((t, n), out_dtype),
        compiler_params=_cparams(("parallel", "arbitrary")),
        name="project",
    )(x, w)


def _merge_kernel(o1, o2, o3, w1, w2, w3, g1, g2, g3, out):
    acc = _sigmoid(g1[...]) * jnp.dot(o1[...], w1[...], preferred_element_type=F32)
    acc += _sigmoid(g2[...]) * jnp.dot(o2[...], w2[...], preferred_element_type=F32)
    acc += _sigmoid(g3[...]) * jnp.dot(o3[...], w3[...], preferred_element_type=F32)
    out[...] = acc.astype(out.dtype)


def merge_branches(ys, w_branch, layer, u_gate):
    t, kb = ys[0].shape
    d = w_branch.shape[3]
    tm, tn = _row_tile(t, 520), _col_tile(d, 512)
    nb = d // tn
    y_spec = pl.BlockSpec((tm, kb), lambda i, j: (i, 0))
    w_specs = [pl.BlockSpec((None, None, kb, tn), functools.partial(lambda i, j, b: (layer, b, 0, j), b=b))
               for b in range(3)]
    g_specs = [pl.BlockSpec((tm, tn), functools.partial(lambda i, j, b: (i, b * nb + j), b=b))
               for b in range(3)]
    return pl.pallas_call(
        _merge_kernel,
        grid=(t // tm, nb),
        in_specs=[y_spec, y_spec, y_spec] + w_specs + g_specs,
        out_specs=pl.BlockSpec((tm, tn), lambda i, j: (i, j)),
        out_shape=jax.ShapeDtypeStruct((t, d), BF16),
        compiler_params=_cparams(("parallel", "arbitrary")),
        name="merge_branches",
    )(ys[0], ys[1], ys[2], w_branch, w_branch, w_branch, u_gate, u_gate, u_gate)


def _resid_kernel(a_ref, w_ref, x_ref, o_ref):
    o_ref[...] = x_ref[...] + jnp.dot(a_ref[...], w_ref[...], preferred_element_type=F32)


def residual_matmul(a, w, layer, x):
    t, k = a.shape
    n = w.shape[2]
    tm, tn = _row_tile(t, 1040), _col_tile(n, 512)
    return pl.pallas_call(
        _resid_kernel,
        grid=(t // tm, n // tn),
        in_specs=[pl.BlockSpec((tm, k), lambda i, j: (i, 0)),
                  pl.BlockSpec((None, k, tn), lambda i, j: (layer, 0, j)),
                  pl.BlockSpec((tm, tn), lambda i, j: (i, j))],
        out_specs=pl.BlockSpec((tm, tn), lambda i, j: (i, j)),
        out_shape=jax.ShapeDtypeStruct((t, n), F32),
        compiler_params=_cparams(("parallel", "arbitrary")),
        name="residual_matmul",
    )(a, w, x)


def _ff1_kernel(a_ref, w_ref, o_ref):
    h = jnp.maximum(jnp.dot(a_ref[...], w_ref[...], preferred_element_type=F32), 0.0)
    o_ref[...] = (h * h).astype(o_ref.dtype)


def ffn_up(a, w, layer):
    t, k = a.shape
    n = w.shape[2]
    tm, tn = _row_tile(t, 1040), _col_tile(n, 1024)
    return pl.pallas_call(
        _ff1_kernel,
        grid=(t // tm, n // tn),
        in_specs=[pl.BlockSpec((tm, k), lambda i, j: (i, 0)),
                  pl.BlockSpec((None, k, tn), lambda i, j: (layer, 0, j))],
        out_specs=pl.BlockSpec((tm, tn), lambda i, j: (i, j)),
        out_shape=jax.ShapeDtypeStruct((t, n), BF16),
        compiler_params=_cparams(("parallel", "arbitrary")),
        name="ffn_up",
    )(a, w)


def _ff2_kernel(a_ref, w_ref, x_ref, o_ref, acc_ref):
    kk = pl.program_id(2)

    @pl.when(kk == 0)
    def _():
        acc_ref[...] = x_ref[...]

    acc_ref[...] += jnp.dot(a_ref[...], w_ref[...], preferred_element_type=F32)

    @pl.when(kk == pl.num_programs(2) - 1)
    def _():
        o_ref[...] = acc_ref[...]


def ffn_down(a, w, layer, x):
    t, k = a.shape
    n = w.shape[2]
    tm, tn, tk = _row_tile(t, 1040), _col_tile(n, 1024), _col_tile(k, 2048)
    return pl.pallas_call(
        _ff2_kernel,
        grid=(t // tm, n // tn, k // tk),
        in_specs=[pl.BlockSpec((tm, tk), lambda i, j, l: (i, l)),
                  pl.BlockSpec((None, tk, tn), lambda i, j, l: (layer, l, j)),
                  pl.BlockSpec((tm, tn), lambda i, j, l: (i, j))],
        out_specs=pl.BlockSpec((tm, tn), lambda i, j, l: (i, j)),
        out_shape=jax.ShapeDtypeStruct((t, n), F32),
        scratch_shapes=[pltpu.VMEM((tm, tn), F32)],
        compiler_params=_cparams(("parallel", "arbitrary", "arbitrary")),
        name="ffn_down",
    )(a, w, x)


def _ple_kernel(p_ref, wp_ref, h_ref, wg_ref, x_ref, o_ref):
    proj = jnp.dot(p_ref[...], wp_ref[...], preferred_element_type=F32)
    gate = _sigmoid(jnp.dot(h_ref[...], wg_ref[...], preferred_element_type=F32))
    o_ref[...] = x_ref[...] + proj * gate


def ple_update(p, wp, hn, wg, layer, x):
    t, kp = p.shape[1:]
    k = hn.shape[1]
    n = wg.shape[2]
    tm, tn = _row_tile(t, 1040), _col_tile(n, 512)
    return pl.pallas_call(
        _ple_kernel,
        grid=(t // tm, n // tn),
        in_specs=[pl.BlockSpec((None, tm, kp), lambda i, j: (layer, i, 0)),
                  pl.BlockSpec((None, kp, tn), lambda i, j: (layer, 0, j)),
                  pl.BlockSpec((tm, k), lambda i, j: (i, 0)),
                  pl.BlockSpec((None, k, tn), lambda i, j: (layer, 0, j)),
                  pl.BlockSpec((tm, tn), lambda i, j: (i, j))],
        out_specs=pl.BlockSpec((tm, tn), lambda i, j: (i, j)),
        out_shape=jax.ShapeDtypeStruct((t, n), F32),
        compiler_params=_cparams(("parallel", "arbitrary")),
        name="ple_update",
    )(p, wp, hn, wg, x)


def _gla_gates(fl, fup_ref, fb_ref):
    return _log_sigmoid(_bdot(fl, fup_ref[...]) + fb_ref[...]) * (1.0 / GLA_TAU)


def _gla_finish(o, g, nw_ref):
    ms = jnp.mean(o * o, axis=-1, keepdims=True)
    return o * lax.rsqrt(ms + EPS) * nw_ref[...] * _silu(g)


def _gla_prompt_kernel(u_ref, fup_ref, fb_ref, nw_ref, y_ref, s_ref):
    c = pl.program_id(1)

    @pl.when(c == 0)
    def _():
        s_ref[...] = jnp.zeros_like(s_ref)

    lg = _gla_gates(u_ref[:, GL_F:GL_F + 128], fup_ref, fb_ref)
    tri = _tri_incl(CH)
    bcum = _hdot(tri, lg)
    blast = bcum[CH - 1:CH, :]
    e_pos = jnp.exp(bcum)
    e_neg = jnp.exp(-bcum)
    e_end = jnp.exp(blast - bcum)
    for h in range(GLA_H):
        ks = slice(h * GLA_DK, (h + 1) * GLA_DK)
        vs = slice(h * GLA_DV, (h + 1) * GLA_DV)
        q = u_ref[:, GL_Q + h * GLA_DK:GL_Q + (h + 1) * GLA_DK] * (GLA_DK ** -0.5)
        k = u_ref[:, GL_K + h * GLA_DK:GL_K + (h + 1) * GLA_DK]
        v = u_ref[:, GL_V + h * GLA_DV:GL_V + (h + 1) * GLA_DV]
        g = u_ref[:, GL_G + h * GLA_DV:GL_G + (h + 1) * GLA_DV]
        q_in = q * e_pos[:, ks]
        k_in = k * e_neg[:, ks]
        k_end = k * e_end[:, ks]
        att = _bdot_nt(q_in, k_in) * tri
        s_prev = s_ref[h]
        o = _bdot(att, v) + _bdot(q_in, s_prev)
        dec = jnp.exp(_hdot_tn(lg[:, ks], jnp.ones((CH, GLA_DV), F32)))
        s_ref[h] = s_prev * dec + _bdot_tn(k_end, v)
        y_ref[:, vs] = _gla_finish(o, g, nw_ref).astype(y_ref.dtype)


def gla_prompt(u, n_seq, seq_len, f_up, f_bias, norm_w):
    nc = seq_len // CH
    y, s = pl.pallas_call(
        _gla_prompt_kernel,
        grid=(n_seq, nc),
        in_specs=[pl.BlockSpec((CH, GL_PAD), lambda b, c: (b * nc + c, 0)),
                  pl.BlockSpec((128, GLA_K), lambda b, c: (0, 0)),
                  pl.BlockSpec((1, GLA_K), lambda b, c: (0, 0)),
                  pl.BlockSpec((1, GLA_DV), lambda b, c: (0, 0))],
        out_specs=[pl.BlockSpec((CH, BR), lambda b, c: (b * nc + c, 0)),
                   pl.BlockSpec((None, GLA_H, GLA_DK, GLA_DV), lambda b, c: (b, 0, 0, 0))],
        out_shape=[jax.ShapeDtypeStruct((n_seq * seq_len, BR), BF16),
                   jax.ShapeDtypeStruct((n_seq, GLA_H, GLA_DK, GLA_DV), F32)],
        compiler_params=_cparams(("parallel", "arbitrary")),
        name="gla_prompt",
    )(u, f_up, f_bias, norm_w)
    return y, s


def _col_of_row(row, eye):
    return jnp.sum(eye * row, axis=-1, keepdims=True)


def _gla_sample_kernel(pids, q_ref, k_ref, v_ref, g_ref, fl_ref, s_ref, fup_ref, fb_ref, nw_ref, y_ref, so_ref):
    nb = q_ref.shape[0]
    alpha = jnp.exp(_gla_gates(fl_ref[...], fup_ref, fb_ref))
    r = lax.broadcasted_iota(jnp.int32, (GLA_DK, GLA_DK), 0)
    c = lax.broadcasted_iota(jnp.int32, (GLA_DK, GLA_DK), 1)
    eye = (r == c).astype(F32)
    for b in range(nb):
        q_col = _col_of_row(q_ref[b:b + 1, :] * (GLA_DK ** -0.5), eye)
        k_col = _col_of_row(k_ref[b:b + 1, :], eye)
        a_col = _col_of_row(alpha[b:b + 1, :], eye)
        s_new = s_ref[b] * a_col + k_col * v_ref[b:b + 1, :]
        so_ref[b] = s_new
        o = jnp.sum(q_col * s_new, axis=0, keepdims=True)
        y_ref[b:b + 1, :] = _gla_finish(o, g_ref[b:b + 1, :], nw_ref)


def _state_chain(kernel_fn, n_in, grid_rank, depth, prev):
    n_skip = 0 if prev is None else 1

    def chained(*refs):
        state_out = refs[n_in + n_skip + 1]
        pids = [pl.program_id(a) for a in range(grid_rank + 1)]

        @pl.when(pids[grid_rank] == 0)
        def _():
            kernel_fn(pids, *refs[:n_in], *refs[n_in + n_skip:])

        @pl.when(pids[grid_rank] != 0)
        def _():
            state_out[...] = jnp.zeros_like(state_out)

    if prev is None:
        return chained, [], [], {}, depth
    return chained, [pl.BlockSpec(memory_space=pl.ANY)], [prev], {n_in: 1}, 1


def gla_sample(u, row0, n_tok, state, layer, prev, f_up, f_bias, norm_w):
    nb = 8
    r0 = row0 // nb
    col = lambda base, w: (lambda i, h, l: (r0 + i, base // w + h))
    kern, xspec, xarg, alias, nl = _state_chain(_gla_sample_kernel, 9, 2, state.shape[0], prev)
    y, s = pl.pallas_call(
        kern,
        grid=(n_tok // nb, GLA_H, nl),
        input_output_aliases=alias,
        in_specs=[pl.BlockSpec((nb, GLA_DK), col(GL_Q, GLA_DK)),
                  pl.BlockSpec((nb, GLA_DK), col(GL_K, GLA_DK)),
                  pl.BlockSpec((nb, GLA_DV), col(GL_V, GLA_DV)),
                  pl.BlockSpec((nb, GLA_DV), col(GL_G, GLA_DV)),
                  pl.BlockSpec((nb, 128), lambda i, h, l: (r0 + i, GL_F // 128)),
                  pl.BlockSpec((None, nb, None, GLA_DK, GLA_DV), lambda i, h, l: (layer, i, h, 0, 0)),
                  pl.BlockSpec((128, GLA_DK), lambda i, h, l: (0, h)),
                  pl.BlockSpec((1, GLA_DK), lambda i, h, l: (0, h)),
                  pl.BlockSpec((1, GLA_DV), lambda i, h, l: (0, 0))] + xspec,
        out_specs=[pl.BlockSpec((nb, GLA_DV), lambda i, h, l: (i, h)),
                   pl.BlockSpec((None, nb, None, GLA_DK, GLA_DV), lambda i, h, l: (layer + l, i, h, 0, 0))],
        out_shape=[jax.ShapeDtypeStruct((n_tok, BR), F32),
                   jax.ShapeDtypeStruct(state.shape, F32)],
        compiler_params=_cparams(("parallel", "parallel", "arbitrary")),
        name="gla_sample",
    )(u, u, u, u, u, state, f_up, f_bias, norm_w, *xarg)
    return y, s


def _head_expand_matrix():
    r = lax.broadcasted_iota(jnp.int32, (128, BR), 0)
    c = lax.broadcasted_iota(jnp.int32, (128, BR), 1)
    return (r == c // HD).astype(F32)


def _stack4(x, mask4):
    return jnp.concatenate([x, x, x, x], axis=0) * mask4


def _quad_masks():
    r = lax.broadcasted_iota(jnp.int32, (4 * CH, QW), 0)
    c = lax.broadcasted_iota(jnp.int32, (4 * CH, QW), 1)
    return (r // CH == c // HD).astype(F32)


def _ssm_prompt_kernel(u_ref, cw_ref, cb_ref, dtb_ref, a_ref, d_ref, nw_ref, y_ref, h_ref,
                       carry_ref, x_s, dtl_s, cum_s, lal_s, yo_s):
    c = pl.program_id(1)

    @pl.when(c == 0)
    def _():
        h_ref[...] = jnp.zeros_like(h_ref)
        carry_ref[...] = jnp.zeros_like(carry_ref)

    pre = u_ref[:, SS_X:SS_X + SSM_XBC]
    prev = carry_ref[...]
    row8 = lax.broadcasted_iota(jnp.int32, (8, SSM_XBC), 0)
    acc = cb_ref[...] + pre * cw_ref[SSM_CONV - 1:SSM_CONV, :]
    for j in range(1, SSM_CONV):
        sh = pltpu.roll(pre, j, axis=0)
        top = jnp.where(row8 < j, pltpu.roll(prev, j, axis=0), sh[0:8, :])
        sh = jnp.concatenate([top, sh[8:, :]], axis=0)
        acc = acc + sh * cw_ref[SSM_CONV - 1 - j:SSM_CONV - j, :]
    carry_ref[...] = pre[CH - 8:CH, :]
    xbc = _silu(acc)
    x_s[...] = xbc[:, 0:BR]

    expand = _head_expand_matrix()
    tri = _tri_incl(CH)
    dt = _softplus(u_ref[:, SS_DT:SS_DT + 128] + dtb_ref[...])
    la = dt * a_ref[...]
    dtl_s[...] = _hdot(dt, expand)
    lal = _hdot(la, expand)
    lal_s[...] = lal
    cum_s[...] = _hdot(tri, lal)

    mask4 = _quad_masks()
    t_i = lax.broadcasted_iota(jnp.int32, (CH, 4 * CH), 0)
    s_i = lax.broadcasted_iota(jnp.int32, (CH, 4 * CH), 1) % CH
    causal = s_i <= t_i
    upto = (t_i <= s_i).astype(F32)
    ones_cn = jnp.ones((CH, SSM_N), F32)
    for q in range(NQ):
        g = q // (NQ // SSM_G)
        ls = slice(q * QW, (q + 1) * QW)
        bm = xbc[:, BR + g * SSM_N:BR + (g + 1) * SSM_N]
        cm = xbc[:, BR + SSM_G * SSM_N + g * SSM_N:BR + SSM_G * SSM_N + (g + 1) * SSM_N]
        cum_q = cum_s[:, ls]
        lal_q = lal_s[:, ls]
        xdt = x_s[:, ls] * dtl_s[:, ls]
        cum_row = jnp.sum(lal_q * upto, axis=0, keepdims=True)
        lmat = jnp.exp(jnp.where(causal, cum_q - cum_row, -1e30))
        cb = _bdot_nt(cm, jnp.concatenate([bm, bm, bm, bm], axis=0))
        y_diag = _bdot(cb * lmat, _stack4(xdt, mask4))
        total = cum_q[CH - 1:CH, :]
        h_prev = h_ref[q * QW:(q + 1) * QW, :]
        y_off = _bdot_nt(cm, h_prev) * jnp.exp(cum_q)
        dec = jnp.exp(_hdot_tn(lal_q, ones_cn))
        h_ref[q * QW:(q + 1) * QW, :] = h_prev * dec + _bdot_tn(xdt * jnp.exp(total - cum_q), bm)
        yo_s[:, ls] = y_diag + y_off + d_ref[:, ls] * x_s[:, ls]

    gw = BR // SSM_G
    for g in range(SSM_G):
        gs = slice(g * gw, (g + 1) * gw)
        yg = yo_s[:, gs] * _silu(u_ref[:, SS_Z + g * gw:SS_Z + (g + 1) * gw])
        ms = jnp.mean(yg * yg, axis=-1, keepdims=True)
        y_ref[:, gs] = (yg * lax.rsqrt(ms + EPS) * nw_ref[:, gs]).astype(y_ref.dtype)


def ssm_prompt(u, n_seq, seq_len, conv_w, conv_b, dt_bias, a_neg, d_lane, norm_w):
    nc = seq_len // CH
    const = lambda shape: pl.BlockSpec(shape, lambda b, c: (0, 0))
    y, h = pl.pallas_call(
        _ssm_prompt_kernel,
        grid=(n_seq, nc),
        in_specs=[pl.BlockSpec((CH, SS_PAD), lambda b, c: (b * nc + c, 0)),
                  const((SSM_CONV, SSM_XBC)), const((1, SSM_XBC)), const((1, 128)), const((1, 128)),
                  const((1, BR)), const((1, BR))],
        out_specs=[pl.BlockSpec((CH, BR), lambda b, c: (b * nc + c, 0)),
                   pl.BlockSpec((None, BR, SSM_N), lambda b, c: (b, 0, 0))],
        out_shape=[jax.ShapeDtypeStruct((n_seq * seq_len, BR), BF16),
                   jax.ShapeDtypeStruct((n_seq, BR, SSM_N), F32)],
        scratch_shapes=[pltpu.VMEM((8, SSM_XBC), F32)] + [pltpu.VMEM((CH, BR), F32)] * 5,
        compiler_params=_cparams(("parallel", "arbitrary")),
        name="ssm_prompt",
    )(u, conv_w, conv_b, dt_bias, a_neg, d_lane, norm_w)
    return y, h


def _ssm_sample_kernel(pids, x_ref, b_ref, c_ref, dt_ref, z_ref, sx_ref, sb_ref, sc_ref,
                       cwx_ref, cwb_ref, cwc_ref, cbx_ref, cbb_ref, cbc_ref,
                       dtb_ref, a_ref, d_ref, nw_ref, h_ref, y_ref, ho_ref):
    nb = x_ref.shape[0]
    gw = x_ref.shape[1]

    def conv(cur_ref, st_ref, w_ref, bias_ref):
        acc = bias_ref[...] + cur_ref[...] * w_ref[SSM_CONV - 1:SSM_CONV, :]
        for j in range(SSM_CONV - 1):
            acc = acc + st_ref[j] * w_ref[j:j + 1, :]
        return _silu(acc)

    xs = conv(x_ref, sx_ref, cwx_ref, cbx_ref)
    bm = conv(b_ref, sb_ref, cwb_ref, cbb_ref)
    cm = conv(c_ref, sc_ref, cwc_ref, cbc_ref)
    g = pids[1]
    r = lax.broadcasted_iota(jnp.int32, (128, gw), 0)
    col = lax.broadcasted_iota(jnp.int32, (128, gw), 1)
    expand = (r == col // HD + g * (gw // HD)).astype(F32)
    dt = _softplus(dt_ref[...] + dtb_ref[...])
    dtl = _hdot(dt, expand)
    decl = jnp.exp(_hdot(dt * a_ref[...], expand))
    xdt = xs * dtl
    rr = lax.broadcasted_iota(jnp.int32, (gw, gw), 0)
    cc = lax.broadcasted_iota(jnp.int32, (gw, gw), 1)
    eye = (rr == cc).astype(F32)
    for b in range(nb):
        dec_col = _col_of_row(decl[b:b + 1, :], eye)
        xdt_col = _col_of_row(xdt[b:b + 1, :], eye)
        h_new = h_ref[b] * dec_col + xdt_col * bm[b:b + 1, :]
        ho_ref[b] = h_new
        y_col = jnp.sum(h_new * cm[b:b + 1, :], axis=-1, keepdims=True)
        y_row = jnp.sum(eye * y_col, axis=0, keepdims=True)
        y_row = (y_row + d_ref[...] * xs[b:b + 1, :]) * _silu(z_ref[b:b + 1, :])
        ms = jnp.mean(y_row * y_row, axis=-1, keepdims=True)
        y_ref[b:b + 1, :] = y_row * lax.rsqrt(ms + EPS) * nw_ref[...]


def ssm_sample(u, row0, n_tok, state, conv_state_t, layer, prev, conv_w, conv_b, dt_bias, a_neg, d_lane, norm_w):
    nb = 8
    kern, xspec, xarg, alias, nl = _state_chain(_ssm_sample_kernel, 19, 2, state.shape[0], prev)
    r0 = row0 // nb
    gw = BR // SSM_G
    bb, cb = BR // SSM_N, (BR + SSM_G * SSM_N) // SSM_N
    ucol = lambda base, w: (lambda i, g, l: (r0 + i, base // w + g))
    scol = lambda base: (lambda i, g, l: (0, i, base + g))
    wcol = lambda base: (lambda i, g, l: (0, base + g))
    y, h = pl.pallas_call(
        kern,
        grid=(n_tok // nb, SSM_G, nl),
        input_output_aliases=alias,
        in_specs=[pl.BlockSpec((nb, gw), ucol(SS_X, gw)),
                  pl.BlockSpec((nb, SSM_N), ucol(SS_B, SSM_N)),
                  pl.BlockSpec((nb, SSM_N), ucol(SS_C, SSM_N)),
                  pl.BlockSpec((nb, 128), lambda i, g, l: (r0 + i, SS_DT // 128)),
                  pl.BlockSpec((nb, gw), ucol(SS_Z, gw)),
                  pl.BlockSpec((SSM_CONV - 1, nb, gw), lambda i, g, l: (0, i, g)),
                  pl.BlockSpec((SSM_CONV - 1, nb, SSM_N), scol(bb)),
                  pl.BlockSpec((SSM_CONV - 1, nb, SSM_N), scol(cb)),
                  pl.BlockSpec((SSM_CONV, gw), lambda i, g, l: (0, g)),
                  pl.BlockSpec((SSM_CONV, SSM_N), wcol(bb)),
                  pl.BlockSpec((SSM_CONV, SSM_N), wcol(cb)),
                  pl.BlockSpec((1, gw), lambda i, g, l: (0, g)),
                  pl.BlockSpec((1, SSM_N), wcol(bb)),
                  pl.BlockSpec((1, SSM_N), wcol(cb)),
                  pl.BlockSpec((1, 128), lambda i, g, l: (0, 0)),
                  pl.BlockSpec((1, 128), lambda i, g, l: (0, 0)),
                  pl.BlockSpec((1, gw), lambda i, g, l: (0, g)),
                  pl.BlockSpec((1, gw), lambda i, g, l: (0, g)),
                  pl.BlockSpec((None, nb, None, gw, SSM_N), lambda i, g, l: (layer, i, g, 0, 0))] + xspec,
        out_specs=[pl.BlockSpec((nb, gw), lambda i, g, l: (i, g)),
                   pl.BlockSpec((None, nb, None, gw, SSM_N), lambda i, g, l: (layer + l, i, g, 0, 0))],
        out_shape=[jax.ShapeDtypeStruct((n_tok, BR), F32),
                   jax.ShapeDtypeStruct(state.shape, F32)],
        compiler_params=_cparams(("parallel", "parallel", "arbitrary")),
        name="ssm_sample",
    )(u, u, u, u, u, conv_state_t, conv_state_t, conv_state_t, conv_w, conv_w, conv_w,
      conv_b, conv_b, conv_b, dt_bias, a_neg, d_lane, norm_w, state, *xarg)
    return y, h


def _rw_mix_inputs(xs_of, w0_ref, w2_ref, a0_ref, a2_ref, g2_ref):
    r = xs_of(RW_R, BR)
    k = xs_of(RW_K, BR)
    v = xs_of(RW_V, BR)
    w = -_softplus(-(w0_ref[...] + _bdot(jnp.tanh(xs_of(RW_WL, 128)), w2_ref[...]))) - 0.5
    log_w = -jnp.exp(w)
    a = _sigmoid(a0_ref[...] + _bdot(xs_of(RW_AL, 128), a2_ref[...]))
    g = _bdot(_sigmoid(xs_of(RW_GL, RW_LG)), g2_ref[...])
    return r, k, v, log_w, a, g


def _seg_sum(x, ones_bd):
    hi = x.astype(BF16)
    lo = (x - hi.astype(F32)).astype(BF16)
    return (jnp.dot(hi, ones_bd, preferred_element_type=F32)
            + jnp.dot(lo, ones_bd, preferred_element_type=F32))


def _rw_prompt_kernel(u_ref, mu_ref, w0_ref, w2_ref, a0_ref, a2_ref, g2_ref, kk_ref, ka_ref, rk_ref,
                      lnw_ref, lnb_ref, y_ref, s_ref, carry_ref, r_s, k_s, v_s, lw_s, a_s, g_s):
    c = pl.program_id(1)

    @pl.when(c == 0)
    def _():
        s_ref[...] = jnp.zeros_like(s_ref)
        carry_ref[...] = jnp.zeros_like(carry_ref)

    row8 = lax.broadcasted_iota(jnp.int32, (8, 1), 0)

    def xs_of(c0, width):
        cur = u_ref[:, c0:c0 + width]
        sh = pltpu.roll(cur, 1, axis=0)
        top = jnp.where(row8 < 1, pltpu.roll(carry_ref[:, c0:c0 + width], 1, axis=0), sh[0:8, :])
        prev = jnp.concatenate([top, sh[8:, :]], axis=0)
        return cur + (prev - cur) * mu_ref[:, c0:c0 + width]

    r, k, v, log_w, a, g = _rw_mix_inputs(xs_of, w0_ref, w2_ref, a0_ref, a2_ref, g2_ref)
    r_s[...] = r
    k_s[...] = k
    v_s[...] = v
    lw_s[...] = log_w
    a_s[...] = a
    g_s[...] = g
    carry_ref[...] = u_ref[CH - 8:CH, :]

    tri = _tri_incl(CH)
    mask4 = _quad_masks()
    rb = lax.broadcasted_iota(jnp.int32, (QW, QW), 0)
    cb = lax.broadcasted_iota(jnp.int32, (QW, QW), 1)
    bd = (rb // HD == cb // HD)
    bdf = bd.astype(F32)
    ones_bd = bd.astype(BF16)
    t_i = lax.broadcasted_iota(jnp.int32, (CH, 4 * CH), 0)
    s_i = lax.broadcasted_iota(jnp.int32, (CH, 4 * CH), 1) % CH
    strict = (s_i < t_i).astype(F32)
    incl = (s_i <= t_i).astype(F32)
    ident = (s_i == t_i).astype(F32)

    def blockdiag(x):
        return jnp.concatenate([x, x, x, x], axis=0) * bdf

    qs = range(NQ)
    ls = [slice(q * QW, (q + 1) * QW) for q in qs]
    stk = lambda x: _stack4(x, mask4)
    r_q = [r_s[:, ls[q]] for q in qs]
    v_q = [v_s[:, ls[q]] for q in qs]
    a_q = [a_s[:, ls[q]] for q in qs]
    kk = [k_s[:, ls[q]] * kk_ref[:, ls[q]] for q in qs]
    kn = [_seg_sum(kk[q] * kk[q], ones_bd) for q in qs]
    cum = [_hdot(tri, lw_s[:, ls[q]]) for q in qs]
    kk = [kk[q] / jnp.maximum(jnp.sqrt(kn[q]), 1e-12) for q in qs]
    kmod = [k_s[:, ls[q]] * (1.0 + (a_q[q] - 1.0) * ka_ref[:, ls[q]]) for q in qs]
    bv = [kk[q] * a_q[q] for q in qs]
    total = [cum[q][CH - 1:CH, :] for q in qs]
    e_neg = [jnp.exp(-cum[q]) for q in qs]
    e_end = [jnp.exp(total[q] - cum[q]) for q in qs]
    at = [-kk[q] * jnp.exp(cum[q] - lw_s[:, ls[q]]) for q in qs]
    rt = [r_q[q] * jnp.exp(cum[q]) for q in qs]
    bbar = [bv[q] * e_end[q] for q in qs]
    kbar = [kmod[q] * e_end[q] for q in qs]

    gram = [_bdot_nt(jnp.concatenate([at[q], rt[q]], axis=0),
                     jnp.concatenate([stk(bv[q] * e_neg[q]), stk(kmod[q] * e_neg[q])], axis=0))
            for q in qs]
    a_ab = [gram[q][0:CH, 0:4 * CH] * strict for q in qs]
    a_ak = [gram[q][0:CH, 4 * CH:8 * CH] * strict for q in qs]
    m_rbk = [jnp.concatenate([gram[q][CH:2 * CH, 0:4 * CH] * incl, gram[q][CH:2 * CH, 4 * CH:8 * CH] * incl],
                             axis=1) for q in qs]

    tinv = [ident + a_ab[q] for q in qs]
    p = a_ab
    for _ in range(5):
        p = [_bdot(p[q], blockdiag(p[q])) for q in qs]
        tinv = [tinv[q] + _bdot(tinv[q], blockdiag(p[q])) for q in qs]

    x1 = [_bdot(a_ak[q], stk(v_q[q])) for q in qs]
    a_hat = [_bdot(tinv[q], stk(at[q])) for q in qs]
    w_v = [_bdot(tinv[q], stk(x1[q])) for q in qs]
    q_hat = [rt[q] + _bdot(m_rbk[q][:, 0:4 * CH], stk(a_hat[q])) for q in qs]
    o_loc = [_bdot(m_rbk[q], jnp.concatenate([stk(w_v[q]), stk(v_q[q])], axis=0)) for q in qs]
    s_prev = [s_ref[q] for q in qs]
    o = [_bdot_nt(q_hat[q], s_prev[q]) + o_loc[q] for q in qs]
    g_corr = [_bdot_tn(a_hat[q], bbar[q]) * bdf for q in qs]
    h_new = [_bdot_tn(jnp.concatenate([w_v[q], v_q[q]], axis=0),
                      jnp.concatenate([bbar[q], kbar[q]], axis=0)) * bdf for q in qs]
    s_corr = [_bdot(s_prev[q], g_corr[q]) for q in qs]
    for q in qs:
        s_ref[q] = s_prev[q] * jnp.exp(total[q]) + s_corr[q] + h_new[q]

    mean = [_seg_sum(o[q], ones_bd) * (1.0 / HD) for q in qs]
    bsum = [_seg_sum(r_q[q] * kmod[q] * rk_ref[:, ls[q]], ones_bd) for q in qs]
    d = [o[q] - mean[q] for q in qs]
    var = [_seg_sum(d[q] * d[q], ones_bd) * (1.0 / HD) for q in qs]
    for q in qs:
        on = d[q] * lax.rsqrt(var[q] + RW_LN_EPS) * lnw_ref[:, ls[q]] + lnb_ref[:, ls[q]]
        y_ref[:, ls[q]] = ((on + bsum[q] * v_q[q]) * g_s[:, ls[q]]).astype(y_ref.dtype)


def rwkv_prompt(u, n_seq, seq_len, mu, w0, w2, a0, a2, g2, k_k, k_a, r_k, ln_w, ln_b):
    nc = seq_len // CH
    const = lambda shape: pl.BlockSpec(shape, lambda b, c: (0, 0))
    y, s = pl.pallas_call(
        _rw_prompt_kernel,
        grid=(n_seq, nc),
        in_specs=[pl.BlockSpec((CH, RW_PAD), lambda b, c: (b * nc + c, 0)),
                  const((1, RW_PAD)), const((1, BR)), const((128, BR)), const((1, BR)), const((128, BR)),
                  const((RW_LG, BR)), const((1, BR)), const((1, BR)), const((1, BR)), const((1, BR)),
                  const((1, BR))],
        out_specs=[pl.BlockSpec((CH, BR), lambda b, c: (b * nc + c, 0)),
                   pl.BlockSpec((None, NQ, QW, QW), lambda b, c: (b, 0, 0, 0))],
        out_shape=[jax.ShapeDtypeStruct((n_seq * seq_len, BR), BF16),
                   jax.ShapeDtypeStruct((n_seq, NQ, QW, QW), F32)],
        scratch_shapes=[pltpu.VMEM((8, RW_PAD), F32)] + [pltpu.VMEM((CH, BR), F32)] * 6,
        compiler_params=_cparams(("parallel", "arbitrary")),
        name="rwkv_prompt",
    )(u, mu, w0, w2, a0, a2, g2, k_k, k_a, r_k, ln_w, ln_b)
    return y, s


def _rw_sample_pre_kernel(u_ref, sh_ref, mu_ref, w0_ref, w2_ref, a0_ref, a2_ref, g2_ref,
                          r_o, k_o, v_o, w_o, a_o, g_o):
    def xs_of(c0, width):
        cur = u_ref[:, c0:c0 + width]
        return cur + (sh_ref[:, c0:c0 + width] - cur) * mu_ref[:, c0:c0 + width]

    r, k, v, log_w, a, g = _rw_mix_inputs(xs_of, w0_ref, w2_ref, a0_ref, a2_ref, g2_ref)
    r_o[...] = r
    k_o[...] = k
    v_o[...] = v
    w_o[...] = jnp.exp(log_w)
    a_o[...] = a
    g_o[...] = g


def _rw_sample_step_kernel(pids, r_ref, k_ref, v_ref, w_ref, a_ref, g_ref, kk_ref, ka_ref, rk_ref,
                           lnw_ref, lnb_ref, s_ref, y_ref, so_ref, o_scr):
    for h in range(r_ref.shape[0]):
        r, k, a = r_ref[h], k_ref[h], a_ref[h]
        w = w_ref[h]
        kk = k * kk_ref[h]
        kk = kk / jnp.maximum(jnp.sqrt(jnp.sum(kk * kk, axis=0, keepdims=True)), 1e-12)
        kmod = k * (1.0 + (a - 1.0) * ka_ref[h])
        av = -kk
        bv = kk * a

        def body(vi, carry):
            s = s_ref[h, vi]
            sa = jnp.sum(s * av, axis=0, keepdims=True)
            s_new = s * w + sa * bv + v_ref[h, pl.ds(vi, 1), :] * kmod
            so_ref[h, vi] = s_new
            o_scr[pl.ds(vi, 1), :] = jnp.sum(s_new * r, axis=0, keepdims=True)
            return carry

        lax.fori_loop(0, HD, body, 0)
        o = o_scr[...]
        mean = jnp.mean(o, axis=0, keepdims=True)
        d = o - mean
        var = jnp.mean(d * d, axis=0, keepdims=True)
        on = d * lax.rsqrt(var + RW_LN_EPS) * lnw_ref[h] + lnb_ref[h]
        bonus = jnp.sum(r * kmod * rk_ref[h], axis=0, keepdims=True) * v_ref[h]
        y_ref[h] = (on + bonus) * g_ref[h]


def rwkv_sample(u, row0, n_tok, state_t, shift_prev, layer, prev, mu, w0, w2, a0, a2, g2, k_k, k_a, r_k, ln_w, ln_b):
    const = lambda shape: pl.BlockSpec(shape, lambda i: (0, 0))
    vec = jax.ShapeDtypeStruct((n_tok, BR), F32)
    outs = pl.pallas_call(
        _rw_sample_pre_kernel,
        grid=(1,),
        in_specs=[pl.BlockSpec((n_tok, RW_PAD), lambda i: (row0 // n_tok, 0)),
                  pl.BlockSpec((None, n_tok, RW_PAD), lambda i: (layer, 0, 0)),
                  const((1, RW_PAD)), const((1, BR)), const((128, BR)), const((1, BR)), const((128, BR)),
                  const((RW_LG, BR))],
        out_specs=[pl.BlockSpec((n_tok, BR), lambda i: (0, 0))] * 6,
        out_shape=[vec] * 6,
        compiler_params=_cparams(("arbitrary",)),
        name="rwkv_sample_pre",
    )(u, shift_prev, mu, w0, w2, a0, a2, g2)
    outs = lax.optimization_barrier(outs)
    hb = 2
    vspec = pl.BlockSpec((hb, HD, n_tok), lambda i, l: (i, 0, 0))
    pspec = pl.BlockSpec((hb, HD, 1), lambda i, l: (i, 0, 0))
    kern, xspec, xarg, alias, nl = _state_chain(_rw_sample_step_kernel, 12, 1, state_t.shape[0], prev)
    y, s = pl.pallas_call(
        kern,
        grid=(NH // hb, nl),
        input_output_aliases=alias,
        in_specs=[vspec] * 6 + [pspec] * 5
        + [pl.BlockSpec((None, hb, HD, HD, n_tok), lambda i, l: (layer, i, 0, 0, 0))] + xspec,
        out_specs=[vspec, pl.BlockSpec((None, hb, HD, HD, n_tok), lambda i, l: (layer + l, i, 0, 0, 0))],
        out_shape=[jax.ShapeDtypeStruct((NH, HD, n_tok), F32),
                   jax.ShapeDtypeStruct(state_t.shape, F32)],
        scratch_shapes=[pltpu.VMEM((HD, n_tok), F32)],
        compiler_params=_cparams(("parallel", "arbitrary")),
        name="rwkv_sample_step",
    )(*[t.T.reshape(NH, HD, n_tok) for t in outs],
      *[t.reshape(NH, HD, 1) for t in (k_k, k_a, r_k, ln_w, ln_b)], state_t, *xarg)
    return y.reshape(BR, n_tok).T, s


def _pad_last(t, width):
    return jnp.pad(t, [(0, 0)] * (t.ndim - 1) + [(0, width - t.shape[-1])])


def _rw_to_padded(t):
    r_k_v = t[..., :3 * BR]
    wl = t[..., 3 * BR:3 * BR + RW_LW]
    al = t[..., 3 * BR + RW_LW:3 * BR + RW_LW + RW_LA]
    gl = t[..., 3 * BR + RW_LW + RW_LA:]
    return jnp.concatenate([r_k_v, _pad_last(wl, 128), _pad_last(al, 128), gl], axis=-1)


def _rw_from_padded(t):
    return jnp.concatenate([t[..., :RW_WL], t[..., RW_WL:RW_WL + RW_LW], t[..., RW_AL:RW_AL + RW_LA],
                            t[..., RW_GL:RW_GL + RW_LG]], axis=-1)


def _split_w_in(w_in):
    o1 = RW_COLS
    o2 = o1 + SS_COLS
    o3 = o2 + GL_COLS
    w = w_in.astype(BF16)
    return (_rw_to_padded(w[:, :, :o1]), _pad_last(w[:, :, o1:o2], SS_PAD), _pad_last(w[:, :, o2:o3], GL_PAD),
            w[:, :, o3:])


def _row(v):
    return v.reshape(1, -1).astype(F32)


def kernel(x_prompt, x_sample, p_prompt, p_sample, state_rwkv, state_shift, state_ssm, state_conv, state_gla, norm_mix, w_in, rw_mu, rw_w0, rw_w2, rw_a0, rw_a2, rw_g2, rw_kk, rw_ka, rw_rk, rw_ln_w, rw_ln_b, ssm_conv_w, ssm_conv_b, ssm_dt_bias, ssm_a_log, ssm_d, ssm_norm, gla_f_up, gla_f_bias, gla_norm, w_branch, w_out, norm_ffn, w_ff1, w_ff2, norm_ple, w_ple_gate, w_ple_proj, norm_final):
    depth = w_in.shape[0]
    bp, seq, d = x_prompt.shape
    ns = x_sample.shape[0]
    assert x_sample.shape[1] == 1, "the sample group is decoded one token at a time"
    assert seq % CH == 0 and seq >= 8
    tp = bp * seq
    assert tp % ns == 0 and ns % 8 == 0

    x = jnp.concatenate([x_prompt.reshape(tp, d), x_sample.reshape(ns, d)], axis=0)
    ple = jnp.concatenate([p_prompt.reshape(depth, tp, -1), p_sample.reshape(depth, ns, -1)], axis=1).astype(BF16)
    shift_pad = _rw_to_padded(state_shift)
    rw_state_t = jnp.transpose(state_rwkv, (0, 2, 3, 4, 1))
    ssm_state = state_ssm.reshape(depth, ns, SSM_G, BR // SSM_G, SSM_N)
    conv_t = jnp.transpose(state_conv, (0, 2, 1, 3))
    wb_branch, wb_out, wb_ff1, wb_ff2, wb_gate, wb_proj = (
        w.astype(BF16) for w in (w_branch, w_out, w_ff1, w_ff2, w_ple_gate, w_ple_proj))

    outs = {k: [] for k in ("rw_p", "sh_p", "sh_s", "ss_p", "cv_p", "cv_s", "gl_p")}
    s_rw = s_ss = s_gl = None
    h = rms_norm(x, norm_mix[0], BF16)
    wb_rw, wb_ss, wb_gl, wb_gates = _split_w_in(w_in)
    for i in range(depth):
        u_rw = project(h, wb_rw, i)
        u_ss = project(h, wb_ss, i)
        u_gl = project(h, wb_gl, i)
        u_gate = project(h, wb_gates, i)

        rw_par = (_rw_to_padded(_row(rw_mu[i])), _row(rw_w0[i]),
                  jnp.pad(rw_w2[i], ((0, 128 - RW_LW), (0, 0))).astype(BF16), _row(rw_a0[i]),
                  jnp.pad(rw_a2[i], ((0, 128 - RW_LA), (0, 0))).astype(BF16), rw_g2[i].astype(BF16),
                  _row(rw_kk[i]), _row(rw_ka[i]), _row(rw_rk[i]), _row(rw_ln_w[i]), _row(rw_ln_b[i]))
        y_rw_p, s_bd = rwkv_prompt(u_rw, bp, seq, *rw_par)
        y_rw_s, s_rw = rwkv_sample(u_rw, tp, ns, rw_state_t, shift_pad, i, s_rw, *rw_par)
        sb = s_bd.reshape(bp, NQ, 4, HD, 4, HD)
        outs["rw_p"].append(jnp.stack([sb[:, :, j, :, j, :] for j in range(4)], axis=2).reshape(bp, NH, HD, HD))
        outs["sh_p"].append(_rw_from_padded(u_rw[seq - 1:tp:seq]))
        outs["sh_s"].append(_rw_from_padded(u_rw[tp:]))

        pad128 = lambda v: jnp.pad(v.astype(F32), (0, 128 - v.shape[0])).reshape(1, 128)
        ss_par = (ssm_conv_w[i], _row(ssm_conv_b[i]), pad128(ssm_dt_bias[i]),
                  pad128(-jnp.exp(ssm_a_log[i].astype(F32))), _row(jnp.repeat(ssm_d[i], HD)), _row(ssm_norm[i]))
        y_ss_p, h_p = ssm_prompt(u_ss, bp, seq, *ss_par)
        y_ss_s, s_ss = ssm_sample(u_ss, tp, ns, ssm_state, conv_t[i], i, s_ss, *ss_par)
        outs["ss_p"].append(h_p.reshape(bp, NH, HD, SSM_N))
        outs["cv_p"].append(jnp.stack([u_ss[(b + 1) * seq - (SSM_CONV - 1):(b + 1) * seq, SS_X:SS_X + SSM_XBC]
                                       for b in range(bp)]))
        outs["cv_s"].append(jnp.concatenate([state_conv[i][:, 1:], u_ss[tp:, None, SS_X:SS_X + SSM_XBC]], axis=1))

        gl_par = (jnp.pad(gla_f_up[i], ((0, 128 - GLA_LORA), (0, 0))).astype(BF16), _row(gla_f_bias[i]),
                  _row(gla_norm[i]))
        y_gl_p, g_p = gla_prompt(u_gl, bp, seq, *gl_par)
        y_gl_s, s_gl = gla_sample(u_gl, tp, ns, state_gla, i, s_gl, *gl_par)
        outs["gl_p"].append(g_p)

        ys = [jnp.concatenate([yp, ysm.astype(BF16)], axis=0)
              for yp, ysm in ((y_rw_p, y_rw_s), (y_ss_p, y_ss_s), (y_gl_p, y_gl_s))]
        merged = merge_branches(ys, wb_branch, i, u_gate)
        x = residual_matmul(merged, wb_out, i, x)
        hf = rms_norm(x, norm_ffn[i], BF16)
        x = ffn_down(ffn_up(hf, wb_ff1, i), wb_ff2, i, x)
        hn = rms_norm(x, norm_ple[i], BF16)
        x = ple_update(ple, wb_proj, hn, wb_gate, i, x)
        if i + 1 < depth:
            h = rms_norm(x, norm_mix[i + 1], BF16)

    y = rms_norm(x, norm_final, F32)
    st = lambda k: jnp.stack(outs[k])
    return (y[:tp].reshape(bp, seq, d), y[tp:].reshape(ns, 1, d), st("rw_p"),
            jnp.transpose(s_rw, (0, 4, 1, 2, 3)), st("sh_p"), st("sh_s"),
            st("ss_p"), s_ss.reshape(depth, ns, NH, HD, SSM_N), st("cv_p"), st("cv_s"), st("gl_p"), s_gl)
```

```python
import functools
import math

import jax
import jax.numpy as jnp
from jax import lax
from jax.experimental import pallas as pl
from jax.experimental.pallas import tpu as pltpu

F32 = jnp.float32
BF16 = jnp.bfloat16

BR = 2048
HD = 64
NH = BR // HD
QW = 256
NQ = BR // QW
CH = 64
RW_LW, RW_LA, RW_LG = 96, 96, 256
RW_LN_EPS = 64e-5
SSM_G, SSM_N, SSM_CONV = 4, 128, 4
SSM_XBC = BR + 2 * SSM_G * SSM_N
GLA_H, GLA_DK, GLA_DV, GLA_LORA, GLA_TAU = 4, 256, 512, 16, 16.0
GLA_K = GLA_H * GLA_DK
EPS = 1e-6

RW_R, RW_K, RW_V, RW_WL, RW_AL, RW_GL, RW_PAD = 0, 2048, 4096, 6144, 6272, 6400, 6656
RW_COLS = 3 * BR + RW_LW + RW_LA + RW_LG
SS_Z, SS_X, SS_B, SS_C, SS_DT, SS_PAD = 0, 2048, 4096, 4608, 5120, 5376
SS_COLS = BR + SSM_XBC + NH
GL_Q, GL_K, GL_V, GL_G, GL_F, GL_PAD = 0, 1024, 2048, 4096, 6144, 6400
GL_COLS = 2 * GLA_K + 2 * BR + GLA_LORA

V7X_VMEM_BYTES = 64 * 1024 * 1024
VMEM_REQUEST = 56 * 1024 * 1024
HI = lax.Precision.HIGHEST


def _cparams(sem):
    return pltpu.CompilerParams(dimension_semantics=sem, vmem_limit_bytes=VMEM_REQUEST)


def _row_tile(n, cap, align=16):
    best = None
    for t in range(align, min(n, cap) + 1, align):
        if n % t == 0:
            best = t
    return best if best is not None else n


def _col_tile(n, cap=1280):
    best = None
    for t in range(256, min(n, cap) + 1, 256):
        if n % t == 0:
            best = t
    if best is None:
        for t in range(128, min(n, cap) + 1, 128):
            if n % t == 0:
                best = t
    return best if best is not None else n


def _bdot(a, b):
    return jnp.dot(a.astype(BF16), b.astype(BF16), preferred_element_type=F32)


def _bdot_nt(a, b):
    return lax.dot_general(a.astype(BF16), b.astype(BF16), (((1,), (1,)), ((), ())),
                           preferred_element_type=F32)


def _bdot_tn(a, b):
    return lax.dot_general(a.astype(BF16), b.astype(BF16), (((0,), (0,)), ((), ())),
                           preferred_element_type=F32)


def _hdot(a, b):
    return jnp.dot(a, b, precision=HI, preferred_element_type=F32)


def _hdot_tn(a, b):
    return lax.dot_general(a, b, (((0,), (0,)), ((), ())), precision=HI,
                           preferred_element_type=F32)


def _split3(x):
    p1 = x.astype(BF16)
    r1 = x - p1.astype(F32)
    p2 = r1.astype(BF16)
    p3 = (r1 - p2.astype(F32)).astype(BF16)
    return p1, p2, p3


def _xdot(a, sel):
    sel = sel.astype(BF16)
    p1, p2, p3 = _split3(a)
    return (jnp.dot(p1, sel, preferred_element_type=F32) + jnp.dot(p2, sel, preferred_element_type=F32)
            + jnp.dot(p3, sel, preferred_element_type=F32))


def _xdot_l(sel, b):
    sel = sel.astype(BF16)
    p1, p2, p3 = _split3(b)
    return (jnp.dot(sel, p1, preferred_element_type=F32) + jnp.dot(sel, p2, preferred_element_type=F32)
            + jnp.dot(sel, p3, preferred_element_type=F32))


def _xdot_tn(a, sel):
    sel = sel.astype(BF16)
    dn = (((0,), (0,)), ((), ()))
    p1, p2, p3 = _split3(a)
    return (lax.dot_general(p1, sel, dn, preferred_element_type=F32)
            + lax.dot_general(p2, sel, dn, preferred_element_type=F32)
            + lax.dot_general(p3, sel, dn, preferred_element_type=F32))


def _sigmoid(x):
    return 1.0 / (1.0 + jnp.exp(-x))


def _silu(x):
    return x * _sigmoid(x)


def _softplus(x):
    return jnp.maximum(x, 0.0) + jnp.log(1.0 + jnp.exp(-jnp.abs(x)))


def _log_sigmoid(x):
    return -_softplus(-x)


def _tri_incl(n):
    r = lax.broadcasted_iota(jnp.int32, (n, n), 0)
    c = lax.broadcasted_iota(jnp.int32, (n, n), 1)
    return (c <= r).astype(F32)


def _rms_kernel(x_ref, g_ref, o_ref):
    x = x_ref[...]
    ms = jnp.mean(x * x, axis=-1, keepdims=True)
    o_ref[...] = (x * lax.rsqrt(ms + EPS) * g_ref[...]).astype(o_ref.dtype)


def rms_norm(x, g, out_dtype):
    t, d = x.shape
    tm = _row_tile(t, 520)
    return pl.pallas_call(
        _rms_kernel,
        grid=(t // tm,),
        in_specs=[pl.BlockSpec((tm, d), lambda i: (i, 0)),
                  pl.BlockSpec((1, d), lambda i: (0, 0))],
        out_specs=pl.BlockSpec((tm, d), lambda i: (i, 0)),
        out_shape=jax.ShapeDtypeStruct((t, d), out_dtype),
        compiler_params=_cparams(("parallel",)),
        name="rms_norm",
    )(x, g.reshape(1, d))


def _rms_split_kernel(x_ref, g_ref, head_ref, tail_ref, *, n_head):
    x = x_ref[...]
    ms = jnp.mean(x * x, axis=-1, keepdims=True)
    y = x * lax.rsqrt(ms + EPS) * g_ref[...]
    i = pl.program_id(0)

    @pl.when(i < n_head)
    def _():
        head_ref[...] = y

    @pl.when(i >= n_head)
    def _():
        tail_ref[...] = y


def rms_norm_split(x, g, n_first, tm):
    t, d = x.shape
    nh = n_first // tm
    return pl.pallas_call(
        functools.partial(_rms_split_kernel, n_head=nh),
        grid=(t // tm,),
        in_specs=[pl.BlockSpec((tm, d), lambda i: (i, 0)),
                  pl.BlockSpec((1, d), lambda i: (0, 0))],
        out_specs=[pl.BlockSpec((tm, d), lambda i: (jnp.minimum(i, nh - 1), 0)),
                   pl.BlockSpec((tm, d), lambda i: (jnp.maximum(i - nh, 0), 0))],
        out_shape=[jax.ShapeDtypeStruct((n_first, d), F32), jax.ShapeDtypeStruct((t - n_first, d), F32)],
        compiler_params=_cparams(("arbitrary",)),
        name="rms_norm_split",
    )(x, g.reshape(1, d))


def _proj_kernel(x_ref, w_ref, o_ref):
    o_ref[...] = jnp.dot(x_ref[...], w_ref[...], preferred_element_type=F32).astype(o_ref.dtype)


def project(x, w, layer, out_dtype=F32):
    t, k = x.shape
    n = w.shape[2]
    tm, tn = _row_tile(t, 1040), _col_tile(n)
    return pl.pallas_call(
        _proj_kernel,
        grid=(t // tm, n // tn),
        in_specs=[pl.BlockSpec((tm, k), lambda i, j: (i, 0)),
                  pl.BlockSpec((None, k, tn), lambda i, j: (layer, 0, j))],
        out_specs=pl.BlockSpec((tm, tn), lambda i, j: (i, j)),
        out_shape=jax.ShapeDtypeStruct((t, n), out_dtype),
        compiler_params=_cparams(("parallel", "arbitrary")),
        name="project",
    )(x, w)


def _merge_kernel(o1, o2, o3, w1, w2, w3, g1, g2, g3, out):
    acc = _sigmoid(g1[...]) * jnp.dot(o1[...], w1[...], preferred_element_type=F32)
    acc += _sigmoid(g2[...]) * jnp.dot(o2[...], w2[...], preferred_element_type=F32)
    acc += _sigmoid(g3[...]) * jnp.dot(o3[...], w3[...], preferred_element_type=F32)
    out[...] = acc.astype(out.dtype)


def merge_branches(ys, w_branch, layer, u_gate):
    t, kb = ys[0].shape
    d = w_branch.shape[3]
    tm, tn = _row_tile(t, 1040), _col_tile(d, 512)
    nb = d // tn
    y_spec = pl.BlockSpec((tm, kb), lambda i, j: (i, 0), pipeline_mode=pl.Buffered(1))
    w_specs = [pl.BlockSpec((None, None, kb, tn), functools.partial(lambda i, j, b: (layer, b, 0, j), b=b))
               for b in range(3)]
    g_specs = [pl.BlockSpec((tm, tn), functools.partial(lambda i, j, b: (i, b * nb + j), b=b))
               for b in range(3)]
    return pl.pallas_call(
        _merge_kernel,
        grid=(t // tm, nb),
        in_specs=[y_spec, y_spec, y_spec] + w_specs + g_specs,
        out_specs=pl.BlockSpec((tm, tn), lambda i, j: (i, j)),
        out_shape=jax.ShapeDtypeStruct((t, d), BF16),
        compiler_params=_cparams(("parallel", "arbitrary")),
        name="merge_branches",
    )(ys[0], ys[1], ys[2], w_branch, w_branch, w_branch, u_gate, u_gate, u_gate)


def _resid_kernel(a_ref, w_ref, x_ref, o_ref, wb_ref):
    _cast_weight_once(w_ref, wb_ref)
    o_ref[...] = x_ref[...] + jnp.dot(a_ref[...], wb_ref[...], preferred_element_type=F32)


def residual_matmul(a, w, layer, x):
    t, k = a.shape
    n = w.shape[2]
    tm, tn = _row_tile(t, 1040), _col_tile(n, 512)
    return pl.pallas_call(
        _resid_kernel,
        grid=(n // tn, t // tm),
        in_specs=[pl.BlockSpec((tm, k), lambda j, i: (i, 0)),
                  pl.BlockSpec((None, k, tn), lambda j, i: (layer, 0, j)),
                  pl.BlockSpec((tm, tn), lambda j, i: (i, j))],
        out_specs=pl.BlockSpec((tm, tn), lambda j, i: (i, j)),
        out_shape=jax.ShapeDtypeStruct((t, n), F32),
        scratch_shapes=[pltpu.VMEM((k, tn), BF16)],
        compiler_params=_cparams(("parallel", "arbitrary")),
        name="residual_matmul",
    )(a, w, x)


def _cast_weight_once(w_ref, wb_ref):
    @pl.when(pl.program_id(1) == 0)
    def _():
        wb_ref[...] = w_ref[...].astype(BF16)


def _ff1_kernel(a_ref, w_ref, o_ref, wb_ref):
    _cast_weight_once(w_ref, wb_ref)
    h = jnp.maximum(jnp.dot(a_ref[...], wb_ref[...], preferred_element_type=F32), 0.0)
    o_ref[...] = (h * h).astype(o_ref.dtype)


def ffn_up(a, w, layer):
    t, k = a.shape
    n = w.shape[2]
    tm, tn = _row_tile(t, 1040), _col_tile(n, 512)
    return pl.pallas_call(
        _ff1_kernel,
        grid=(n // tn, t // tm),
        in_specs=[pl.BlockSpec((tm, k), lambda j, i: (i, 0)),
                  pl.BlockSpec((None, k, tn), lambda j, i: (layer, 0, j))],
        out_specs=pl.BlockSpec((tm, tn), lambda j, i: (i, j)),
        out_shape=jax.ShapeDtypeStruct((t, n), BF16),
        scratch_shapes=[pltpu.VMEM((k, tn), BF16)],
        compiler_params=_cparams(("parallel", "arbitrary")),
        name="ffn_up",
    )(a, w)


def _ff2_kernel(a_ref, w_ref, x_ref, o_ref, acc_ref):
    kk = pl.program_id(2)

    @pl.when(kk == 0)
    def _():
        acc_ref[...] = x_ref[...]

    acc_ref[...] += jnp.dot(a_ref[...], w_ref[...], preferred_element_type=F32)

    @pl.when(kk == pl.num_programs(2) - 1)
    def _():
        o_ref[...] = acc_ref[...]


def ffn_down(a, w, layer, x):
    t, k = a.shape
    n = w.shape[2]
    tm, tn, tk = _row_tile(t, 1040), _col_tile(n, 512), _col_tile(k, 4096)
    return pl.pallas_call(
        _ff2_kernel,
        grid=(t // tm, n // tn, k // tk),
        in_specs=[pl.BlockSpec((tm, tk), lambda i, j, l: (i, l)),
                  pl.BlockSpec((None, tk, tn), lambda i, j, l: (layer, l, j)),
                  pl.BlockSpec((tm, tn), lambda i, j, l: (i, j))],
        out_specs=pl.BlockSpec((tm, tn), lambda i, j, l: (i, j)),
        out_shape=jax.ShapeDtypeStruct((t, n), F32),
        scratch_shapes=[pltpu.VMEM((tm, tn), F32)],
        compiler_params=_cparams(("parallel", "arbitrary", "arbitrary")),
        name="ffn_down",
    )(a, w, x)


def _ple_kernel(p_ref, wp_ref, h_ref, wg_ref, x_ref, o_ref, wgb_ref):
    _cast_weight_once(wg_ref, wgb_ref)
    proj = jnp.dot(p_ref[...], wp_ref[...].astype(BF16), preferred_element_type=F32)
    gate = _sigmoid(jnp.dot(h_ref[...], wgb_ref[...], preferred_element_type=F32))
    o_ref[...] = x_ref[...] + proj * gate


def ple_update(p, wp, hn, wg, layer, x):
    t, kp = p.shape[1:]
    k = hn.shape[1]
    n = wg.shape[2]
    tm, tn = _row_tile(t, 1040), _col_tile(n, 512)
    return pl.pallas_call(
        _ple_kernel,
        grid=(n // tn, t // tm),
        in_specs=[pl.BlockSpec((None, tm, kp), lambda j, i: (layer, i, 0)),
                  pl.BlockSpec((None, kp, tn), lambda j, i: (layer, 0, j)),
                  pl.BlockSpec((tm, k), lambda j, i: (i, 0)),
                  pl.BlockSpec((None, k, tn), lambda j, i: (layer, 0, j)),
                  pl.BlockSpec((tm, tn), lambda j, i: (i, j))],
        out_specs=pl.BlockSpec((tm, tn), lambda j, i: (i, j)),
        out_shape=jax.ShapeDtypeStruct((t, n), F32),
        scratch_shapes=[pltpu.VMEM((k, tn), BF16)],
        compiler_params=_cparams(("parallel", "arbitrary")),
        name="ple_update",
    )(p, wp, hn, wg, x)


def _gla_gates(fl, fup_ref, fb_ref):
    return _log_sigmoid(_bdot(fl, fup_ref[...]) + fb_ref[...]) * (1.0 / GLA_TAU)


def _gla_finish(o, g, nw_ref):
    ms = jnp.mean(o * o, axis=-1, keepdims=True)
    return o * lax.rsqrt(ms + EPS) * nw_ref[...] * _silu(g)


def _prompt_rows(n_seq, seq_len, n_rows):
    nc = seq_len // CH
    tail = n_rows - n_seq * seq_len
    assert tail % CH == 0 and tail // CH <= n_seq
    nt = tail // CH
    steps = nc + (1 if nt else 0)
    u_map = lambda b, c: (b * nc + jnp.minimum(c, nc - 1), 0)
    y_map = lambda b, c: (jnp.where(c < nc, b * nc + c, jnp.where(b < nt, n_seq * nc + b, b * nc + nc - 1)), 0)
    return (nc, nt), steps, u_map, y_map


def _chunk_kernel(body, nc_nt, y_index):
    nc, nt = nc_nt

    def kern(*refs):
        b = pl.program_id(0)
        c = pl.program_id(1)

        @pl.when(c < nc)
        def _():
            body(c, *refs)

        @pl.when(jnp.logical_and(c >= nc, b < nt))
        def _():
            refs[y_index][...] = jnp.zeros_like(refs[y_index])

    return kern


def _insert_kernel(rows_ref, buf_ref, o_ref):
    o_ref[...] = rows_ref[...].astype(o_ref.dtype)


def insert_rows(buf, rows, row0):
    n, w = rows.shape
    assert row0 % n == 0
    return pl.pallas_call(
        _insert_kernel,
        grid=(1,),
        in_specs=[pl.BlockSpec((n, w), lambda i: (0, 0)), pl.BlockSpec(memory_space=pl.ANY)],
        out_specs=pl.BlockSpec((n, w), lambda i: (row0 // n, 0)),
        out_shape=jax.ShapeDtypeStruct(buf.shape, buf.dtype),
        input_output_aliases={1: 0},
        compiler_params=_cparams(("arbitrary",)),
        name="insert_rows",
    )(rows, buf)


def _gla_prompt_kernel(c, u_ref, fup_ref, fb_ref, nw_ref, y_ref, s_ref):
    @pl.when(c == 0)
    def _():
        s_ref[...] = jnp.zeros_like(s_ref)

    lg = _gla_gates(u_ref[:, GL_F:GL_F + 128], fup_ref, fb_ref)
    tri = _tri_incl(CH)
    bcum = _xdot_l(tri, lg)
    ones_c = jnp.ones((CH, 128), F32)
    blast = bcum[CH - 1:CH, :]
    e_pos = jnp.exp(bcum)
    e_neg = jnp.exp(-bcum)
    e_end = jnp.exp(blast - bcum)
    for h in range(GLA_H):
        ks = slice(h * GLA_DK, (h + 1) * GLA_DK)
        vs = slice(h * GLA_DV, (h + 1) * GLA_DV)
        q = u_ref[:, GL_Q + h * GLA_DK:GL_Q + (h + 1) * GLA_DK] * (GLA_DK ** -0.5)
        k = u_ref[:, GL_K + h * GLA_DK:GL_K + (h + 1) * GLA_DK]
        v = u_ref[:, GL_V + h * GLA_DV:GL_V + (h + 1) * GLA_DV]
        g = u_ref[:, GL_G + h * GLA_DV:GL_G + (h + 1) * GLA_DV]
        q_in = q * e_pos[:, ks]
        k_in = k * e_neg[:, ks]
        k_end = k * e_end[:, ks]
        att = _bdot_nt(q_in, k_in) * tri
        s_prev = s_ref[h]
        o = _bdot(att, v) + _bdot(q_in, s_prev)
        dec = jnp.exp(_xdot_tn(lg[:, ks], ones_c))
        s_ref[h] = s_prev * jnp.concatenate([dec] * (GLA_DV // 128), axis=1) + _bdot_tn(k_end, v)
        y_ref[:, vs] = _gla_finish(o, g, nw_ref).astype(y_ref.dtype)


def gla_prompt(u, n_seq, seq_len, f_up, f_bias, norm_w):
    nc, steps, u_map, y_map = _prompt_rows(n_seq, seq_len, u.shape[0])
    y, s = pl.pallas_call(
        _chunk_kernel(_gla_prompt_kernel, nc, 4),
        grid=(n_seq, steps),
        in_specs=[pl.BlockSpec((CH, GL_PAD), u_map),
                  pl.BlockSpec((128, GLA_K), lambda b, c: (0, 0)),
                  pl.BlockSpec((1, GLA_K), lambda b, c: (0, 0)),
                  pl.BlockSpec((1, GLA_DV), lambda b, c: (0, 0))],
        out_specs=[pl.BlockSpec((CH, BR), y_map),
                   pl.BlockSpec((None, GLA_H, GLA_DK, GLA_DV), lambda b, c: (b, 0, 0, 0))],
        out_shape=[jax.ShapeDtypeStruct((u.shape[0], BR), BF16),
                   jax.ShapeDtypeStruct((n_seq, GLA_H, GLA_DK, GLA_DV), F32)],
        compiler_params=_cparams(("parallel", "arbitrary")),
        name="gla_prompt",
    )(u, f_up, f_bias, norm_w)
    return y, s


def _col_of_row(row, eye):
    return jnp.sum(eye * row, axis=-1, keepdims=True)


def _gla_sample_kernel(q_ref, k_ref, v_ref, g_ref, fl_ref, s_ref, fup_ref, fb_ref, nw_ref, y_ref, so_ref):
    nb = q_ref.shape[0]
    alpha = jnp.exp(_gla_gates(fl_ref[...], fup_ref, fb_ref))
    r = lax.broadcasted_iota(jnp.int32, (GLA_DK, GLA_DK), 0)
    c = lax.broadcasted_iota(jnp.int32, (GLA_DK, GLA_DK), 1)
    eye = (r == c).astype(F32)
    for b in range(nb):
        q_col = _col_of_row(q_ref[b:b + 1, :] * (GLA_DK ** -0.5), eye)
        k_col = _col_of_row(k_ref[b:b + 1, :], eye)
        a_col = _col_of_row(alpha[b:b + 1, :], eye)
        s_new = s_ref[b] * a_col + k_col * v_ref[b:b + 1, :]
        so_ref[b] = s_new
        o = jnp.sum(q_col * s_new, axis=0, keepdims=True)
        y_ref[b:b + 1, :] = _gla_finish(o, g_ref[b:b + 1, :], nw_ref)


def _state_chain(kernel_fn, n_in, depth, prev):
    if prev is None:
        def first(*refs):
            state_out = refs[n_in + 1]
            kernel_fn(*refs[:n_in + 1], state_out.at[0], *refs[n_in + 2:])
            for later in range(1, depth):
                state_out[later] = jnp.zeros(state_out.shape[1:], state_out.dtype)

        return first, [], [], {}, depth

    def chained(*refs):
        kernel_fn(*refs[:n_in], *refs[n_in + 1:])

    return chained, [pl.BlockSpec(memory_space=pl.ANY)], [prev], {n_in: 1}, None


def gla_sample(u, row0, n_tok, state, layer, prev, f_up, f_bias, norm_w):
    nb = 8
    r0 = row0 // nb
    col = lambda base, w: (lambda i, h: (r0 + i, base // w + h))
    kern, xspec, xarg, alias, lead = _state_chain(_gla_sample_kernel, 9, state.shape[0], prev)
    y, s = pl.pallas_call(
        kern,
        grid=(n_tok // nb, GLA_H),
        input_output_aliases=alias,
        in_specs=[pl.BlockSpec((nb, GLA_DK), col(GL_Q, GLA_DK)),
                  pl.BlockSpec((nb, GLA_DK), col(GL_K, GLA_DK)),
                  pl.BlockSpec((nb, GLA_DV), col(GL_V, GLA_DV)),
                  pl.BlockSpec((nb, GLA_DV), col(GL_G, GLA_DV)),
                  pl.BlockSpec((nb, 128), lambda i, h: (r0 + i, GL_F // 128)),
                  pl.BlockSpec((None, nb, None, GLA_DK, GLA_DV), lambda i, h: (layer, i, h, 0, 0)),
                  pl.BlockSpec((128, GLA_DK), lambda i, h: (0, h)),
                  pl.BlockSpec((1, GLA_DK), lambda i, h: (0, h)),
                  pl.BlockSpec((1, GLA_DV), lambda i, h: (0, 0))] + xspec,
        out_specs=[pl.BlockSpec((nb, GLA_DV), lambda i, h: (i, h)),
                   pl.BlockSpec((lead, nb, None, GLA_DK, GLA_DV),
                                lambda i, h: (0 if lead else layer, i, h, 0, 0))],
        out_shape=[jax.ShapeDtypeStruct((n_tok, BR), F32),
                   jax.ShapeDtypeStruct(state.shape, F32)],
        compiler_params=_cparams(("parallel", "parallel")),
        name="gla_sample",
    )(u, u, u, u, u, state, f_up, f_bias, norm_w, *xarg)
    return y, s


def _head_expand_matrix():
    r = lax.broadcasted_iota(jnp.int32, (128, BR), 0)
    c = lax.broadcasted_iota(jnp.int32, (128, BR), 1)
    return (r == c // HD).astype(F32)


def _stack4(x, mask4):
    return jnp.concatenate([x, x, x, x], axis=0) * mask4


def _quad_masks():
    r = lax.broadcasted_iota(jnp.int32, (4 * CH, QW), 0)
    c = lax.broadcasted_iota(jnp.int32, (4 * CH, QW), 1)
    return (r // CH == c // HD).astype(F32)


def _ssm_prompt_kernel(c, u_ref, cw_ref, cb_ref, dtb_ref, a_ref, d_ref, nw_ref, y_ref, h_ref,
                       carry_ref, x_s, dtl_s, cum_s, lal_s, yo_s):
    @pl.when(c == 0)
    def _():
        h_ref[...] = jnp.zeros_like(h_ref)
        carry_ref[...] = jnp.zeros_like(carry_ref)

    pre = u_ref[:, SS_X:SS_X + SSM_XBC]
    prev = carry_ref[...]
    row8 = lax.broadcasted_iota(jnp.int32, (8, SSM_XBC), 0)
    acc = cb_ref[...] + pre * cw_ref[SSM_CONV - 1:SSM_CONV, :]
    for j in range(1, SSM_CONV):
        sh = pltpu.roll(pre, j, axis=0)
        top = jnp.where(row8 < j, pltpu.roll(prev, j, axis=0), sh[0:8, :])
        sh = jnp.concatenate([top, sh[8:, :]], axis=0)
        acc = acc + sh * cw_ref[SSM_CONV - 1 - j:SSM_CONV - j, :]
    carry_ref[...] = pre[CH - 8:CH, :]
    xbc = _silu(acc)
    x_s[...] = xbc[:, 0:BR]

    expand = _head_expand_matrix()
    tri = _tri_incl(CH)
    dt = _softplus(u_ref[:, SS_DT:SS_DT + 128] + dtb_ref[...])
    la = dt * a_ref[...]
    dtl_s[...] = _xdot(dt, expand)
    lal_s[...] = _xdot(la, expand)
    cum_s[...] = _xdot(_xdot_l(tri, la), expand)

    mask4 = _quad_masks()
    t_i = lax.broadcasted_iota(jnp.int32, (CH, 4 * CH), 0)
    s_i = lax.broadcasted_iota(jnp.int32, (CH, 4 * CH), 1) % CH
    causal = s_i <= t_i
    upto = (t_i <= s_i).astype(F32)
    ones_cn = jnp.ones((CH, SSM_N), F32)
    for q in range(NQ):
        g = q // (NQ // SSM_G)
        ls = slice(q * QW, (q + 1) * QW)
        bm = xbc[:, BR + g * SSM_N:BR + (g + 1) * SSM_N]
        cm = xbc[:, BR + SSM_G * SSM_N + g * SSM_N:BR + SSM_G * SSM_N + (g + 1) * SSM_N]
        cum_q = cum_s[:, ls]
        lal_q = lal_s[:, ls]
        xdt = x_s[:, ls] * dtl_s[:, ls]
        cum_row = jnp.sum(lal_q * upto, axis=0, keepdims=True)
        lmat = jnp.exp(jnp.where(causal, cum_q - cum_row, -1e30))
        cb = _bdot_nt(cm, jnp.concatenate([bm, bm, bm, bm], axis=0))
        y_diag = _bdot(cb * lmat, _stack4(xdt, mask4))
        total = cum_q[CH - 1:CH, :]
        h_prev = h_ref[q * QW:(q + 1) * QW, :]
        y_off = _bdot_nt(cm, h_prev) * jnp.exp(cum_q)
        dec = jnp.exp(_xdot_tn(lal_q, ones_cn))
        h_ref[q * QW:(q + 1) * QW, :] = h_prev * dec + _bdot_tn(xdt * jnp.exp(total - cum_q), bm)
        yo_s[:, ls] = y_diag + y_off + d_ref[:, ls] * x_s[:, ls]

    gw = BR // SSM_G
    for g in range(SSM_G):
        gs = slice(g * gw, (g + 1) * gw)
        yg = yo_s[:, gs] * _silu(u_ref[:, SS_Z + g * gw:SS_Z + (g + 1) * gw])
        ms = jnp.mean(yg * yg, axis=-1, keepdims=True)
        y_ref[:, gs] = (yg * lax.rsqrt(ms + EPS) * nw_ref[:, gs]).astype(y_ref.dtype)


def ssm_prompt(u, n_seq, seq_len, conv_w, conv_b, dt_bias, a_neg, d_lane, norm_w):
    nc, steps, u_map, y_map = _prompt_rows(n_seq, seq_len, u.shape[0])
    const = lambda shape: pl.BlockSpec(shape, lambda b, c: (0, 0))
    y, h = pl.pallas_call(
        _chunk_kernel(_ssm_prompt_kernel, nc, 7),
        grid=(n_seq, steps),
        in_specs=[pl.BlockSpec((CH, SS_PAD), u_map),
                  const((SSM_CONV, SSM_XBC)), const((1, SSM_XBC)), const((1, 128)), const((1, 128)),
                  const((1, BR)), const((1, BR))],
        out_specs=[pl.BlockSpec((CH, BR), y_map),
                   pl.BlockSpec((None, BR, SSM_N), lambda b, c: (b, 0, 0))],
        out_shape=[jax.ShapeDtypeStruct((u.shape[0], BR), BF16),
                   jax.ShapeDtypeStruct((n_seq, BR, SSM_N), F32)],
        scratch_shapes=[pltpu.VMEM((8, SSM_XBC), F32)] + [pltpu.VMEM((CH, BR), F32)] * 5,
        compiler_params=_cparams(("parallel", "arbitrary")),
        name="ssm_prompt",
    )(u, conv_w, conv_b, dt_bias, a_neg, d_lane, norm_w)
    return y, h


def _ssm_sample_kernel(x_ref, b_ref, c_ref, dt_ref, z_ref, sx_ref, sb_ref, sc_ref,
                       cwx_ref, cwb_ref, cwc_ref, cbx_ref, cbb_ref, cbc_ref,
                       dtb_ref, a_ref, d_ref, nw_ref, h_ref, y_ref, ho_ref):
    nb = x_ref.shape[0]
    gw = x_ref.shape[1]

    def conv(cur_ref, st_ref, w_ref, bias_ref):
        acc = bias_ref[...] + cur_ref[...] * w_ref[SSM_CONV - 1:SSM_CONV, :]
        for j in range(SSM_CONV - 1):
            acc = acc + st_ref[j] * w_ref[j:j + 1, :]
        return _silu(acc)

    xs = conv(x_ref, sx_ref, cwx_ref, cbx_ref)
    bm = conv(b_ref, sb_ref, cwb_ref, cbb_ref)
    cm = conv(c_ref, sc_ref, cwc_ref, cbc_ref)
    g = pl.program_id(1)
    r = lax.broadcasted_iota(jnp.int32, (128, gw), 0)
    col = lax.broadcasted_iota(jnp.int32, (128, gw), 1)
    expand = (r == col // HD + g * (gw // HD)).astype(F32)
    dt = _softplus(dt_ref[...] + dtb_ref[...])
    dtl = _hdot(dt, expand)
    decl = jnp.exp(_hdot(dt * a_ref[...], expand))
    xdt = xs * dtl
    rr = lax.broadcasted_iota(jnp.int32, (gw, gw), 0)
    cc = lax.broadcasted_iota(jnp.int32, (gw, gw), 1)
    eye = (rr == cc).astype(F32)
    for b in range(nb):
        dec_col = _col_of_row(decl[b:b + 1, :], eye)
        xdt_col = _col_of_row(xdt[b:b + 1, :], eye)
        h_new = h_ref[b] * dec_col + xdt_col * bm[b:b + 1, :]
        ho_ref[b] = h_new
        y_col = jnp.sum(h_new * cm[b:b + 1, :], axis=-1, keepdims=True)
        y_row = jnp.sum(eye * y_col, axis=0, keepdims=True)
        y_row = (y_row + d_ref[...] * xs[b:b + 1, :]) * _silu(z_ref[b:b + 1, :])
        ms = jnp.mean(y_row * y_row, axis=-1, keepdims=True)
        y_ref[b:b + 1, :] = y_row * lax.rsqrt(ms + EPS) * nw_ref[...]


def ssm_sample(u, row0, n_tok, state, conv_state_t, layer, prev, conv_w, conv_b, dt_bias, a_neg, d_lane, norm_w):
    nb = 8
    kern, xspec, xarg, alias, lead = _state_chain(_ssm_sample_kernel, 19, state.shape[0], prev)
    r0 = row0 // nb
    gw = BR // SSM_G
    bb, cb = BR // SSM_N, (BR + SSM_G * SSM_N) // SSM_N
    ucol = lambda base, w: (lambda i, g: (r0 + i, base // w + g))
    scol = lambda base: (lambda i, g: (0, i, base + g))
    wcol = lambda base: (lambda i, g: (0, base + g))
    y, h = pl.pallas_call(
        kern,
        grid=(n_tok // nb, SSM_G),
        input_output_aliases=alias,
        in_specs=[pl.BlockSpec((nb, gw), ucol(SS_X, gw)),
                  pl.BlockSpec((nb, SSM_N), ucol(SS_B, SSM_N)),
                  pl.BlockSpec((nb, SSM_N), ucol(SS_C, SSM_N)),
                  pl.BlockSpec((nb, 128), lambda i, g: (r0 + i, SS_DT // 128)),
                  pl.BlockSpec((nb, gw), ucol(SS_Z, gw)),
                  pl.BlockSpec((SSM_CONV - 1, nb, gw), lambda i, g: (0, i, g)),
                  pl.BlockSpec((SSM_CONV - 1, nb, SSM_N), scol(bb)),
                  pl.BlockSpec((SSM_CONV - 1, nb, SSM_N), scol(cb)),
                  pl.BlockSpec((SSM_CONV, gw), lambda i, g: (0, g)),
                  pl.BlockSpec((SSM_CONV, SSM_N), wcol(bb)),
                  pl.BlockSpec((SSM_CONV, SSM_N), wcol(cb)),
                  pl.BlockSpec((1, gw), lambda i, g: (0, g)),
                  pl.BlockSpec((1, SSM_N), wcol(bb)),
                  pl.BlockSpec((1, SSM_N), wcol(cb)),
                  pl.BlockSpec((1, 128), lambda i, g: (0, 0)),
                  pl.BlockSpec((1, 128), lambda i, g: (0, 0)),
                  pl.BlockSpec((1, gw), lambda i, g: (0, g)),
                  pl.BlockSpec((1, gw), lambda i, g: (0, g)),
                  pl.BlockSpec((None, nb, None, gw, SSM_N), lambda i, g: (layer, i, g, 0, 0))] + xspec,
        out_specs=[pl.BlockSpec((nb, gw), lambda i, g: (i, g)),
                   pl.BlockSpec((lead, nb, None, gw, SSM_N), lambda i, g: (0 if lead else layer, i, g, 0, 0))],
        out_shape=[jax.ShapeDtypeStruct((n_tok, BR), F32),
                   jax.ShapeDtypeStruct(state.shape, F32)],
        compiler_params=_cparams(("parallel", "parallel")),
        name="ssm_sample",
    )(u, u, u, u, u, conv_state_t, conv_state_t, conv_state_t, conv_w, conv_w, conv_w,
      conv_b, conv_b, conv_b, dt_bias, a_neg, d_lane, norm_w, state, *xarg)
    return y, h


def _rw_mix_inputs(xs_of, w0_ref, w2_ref, a0_ref, a2_ref, g2_ref):
    r = xs_of(RW_R, BR)
    k = xs_of(RW_K, BR)
    v = xs_of(RW_V, BR)
    w = -_softplus(-(w0_ref[...] + _bdot(jnp.tanh(xs_of(RW_WL, 128)), w2_ref[...]))) - 0.5
    log_w = -jnp.exp(w)
    a = _sigmoid(a0_ref[...] + _bdot(xs_of(RW_AL, 128), a2_ref[...]))
    g = _bdot(_sigmoid(xs_of(RW_GL, RW_LG)), g2_ref[...])
    return r, k, v, log_w, a, g


def _seg_sum(x, ones_bd):
    hi = x.astype(BF16)
    lo = (x - hi.astype(F32)).astype(BF16)
    return (jnp.dot(hi, ones_bd, preferred_element_type=F32)
            + jnp.dot(lo, ones_bd, preferred_element_type=F32))


def _rw_prompt_kernel(c, u_ref, mu_ref, w0_ref, w2_ref, a0_ref, a2_ref, g2_ref, kk_ref, ka_ref, rk_ref,
                      lnw_ref, lnb_ref, y_ref, s_ref, carry_ref, r_s, k_s, v_s, lw_s, a_s, g_s):
    @pl.when(c == 0)
    def _():
        s_ref[...] = jnp.zeros_like(s_ref)
        carry_ref[...] = jnp.zeros_like(carry_ref)

    row8 = lax.broadcasted_iota(jnp.int32, (8, 1), 0)

    def xs_of(c0, width):
        cur = u_ref[:, c0:c0 + width]
        sh = pltpu.roll(cur, 1, axis=0)
        top = jnp.where(row8 < 1, pltpu.roll(carry_ref[:, c0:c0 + width], 1, axis=0), sh[0:8, :])
        prev = jnp.concatenate([top, sh[8:, :]], axis=0)
        return cur + (prev - cur) * mu_ref[:, c0:c0 + width]

    r, k, v, log_w, a, g = _rw_mix_inputs(xs_of, w0_ref, w2_ref, a0_ref, a2_ref, g2_ref)
    r_s[...] = r
    k_s[...] = k
    v_s[...] = v
    lw_s[...] = log_w
    a_s[...] = a
    g_s[...] = g
    carry_ref[...] = u_ref[CH - 8:CH, :]

    tri = _tri_incl(CH)
    mask4 = _quad_masks()
    rb = lax.broadcasted_iota(jnp.int32, (QW, QW), 0)
    cb = lax.broadcasted_iota(jnp.int32, (QW, QW), 1)
    bd = (rb // HD == cb // HD)
    bdf = bd.astype(F32)
    ones_bd = bd.astype(BF16)
    t_i = lax.broadcasted_iota(jnp.int32, (CH, 4 * CH), 0)
    s_i = lax.broadcasted_iota(jnp.int32, (CH, 4 * CH), 1) % CH
    strict = (s_i < t_i).astype(F32)
    incl = (s_i <= t_i).astype(F32)
    ident = (s_i == t_i).astype(F32)

    def blockdiag(x):
        return jnp.concatenate([x, x, x, x], axis=0) * bdf

    qs = range(NQ)
    ls = [slice(q * QW, (q + 1) * QW) for q in qs]
    stk = lambda x: _stack4(x, mask4)
    r_q = [r_s[:, ls[q]] for q in qs]
    v_q = [v_s[:, ls[q]] for q in qs]
    a_q = [a_s[:, ls[q]] for q in qs]
    kk = [k_s[:, ls[q]] * kk_ref[:, ls[q]] for q in qs]
    kn = [_seg_sum(kk[q] * kk[q], ones_bd) for q in qs]
    cum = [_xdot_l(tri, lw_s[:, ls[q]]) for q in qs]
    kk = [kk[q] / jnp.maximum(jnp.sqrt(kn[q]), 1e-12) for q in qs]
    kmod = [k_s[:, ls[q]] * (1.0 + (a_q[q] - 1.0) * ka_ref[:, ls[q]]) for q in qs]
    bv = [kk[q] * a_q[q] for q in qs]
    total = [cum[q][CH - 1:CH, :] for q in qs]
    e_neg = [jnp.exp(-cum[q]) for q in qs]
    e_end = [jnp.exp(total[q] - cum[q]) for q in qs]
    at = [-kk[q] * jnp.exp(cum[q] - lw_s[:, ls[q]]) for q in qs]
    rt = [r_q[q] * jnp.exp(cum[q]) for q in qs]
    bbar = [bv[q] * e_end[q] for q in qs]
    kbar = [kmod[q] * e_end[q] for q in qs]

    gram = [_bdot_nt(jnp.concatenate([at[q], rt[q]], axis=0),
                     jnp.concatenate([stk(bv[q] * e_neg[q]), stk(kmod[q] * e_neg[q])], axis=0))
            for q in qs]
    a_ab = [gram[q][0:CH, 0:4 * CH] * strict for q in qs]
    a_ak = [gram[q][0:CH, 4 * CH:8 * CH] * strict for q in qs]
    m_rbk = [jnp.concatenate([gram[q][CH:2 * CH, 0:4 * CH] * incl, gram[q][CH:2 * CH, 4 * CH:8 * CH] * incl],
                             axis=1) for q in qs]

    tinv = [ident + a_ab[q] for q in qs]
    p = a_ab
    for _ in range(5):
        p = [_bdot(p[q], blockdiag(p[q])) for q in qs]
        tinv = [tinv[q] + _bdot(tinv[q], blockdiag(p[q])) for q in qs]

    x1 = [_bdot(a_ak[q], stk(v_q[q])) for q in qs]
    a_hat = [_bdot(tinv[q], stk(at[q])) for q in qs]
    w_v = [_bdot(tinv[q], stk(x1[q])) for q in qs]
    q_hat = [rt[q] + _bdot(m_rbk[q][:, 0:4 * CH], stk(a_hat[q])) for q in qs]
    o_loc = [_bdot(m_rbk[q], jnp.concatenate([stk(w_v[q]), stk(v_q[q])], axis=0)) for q in qs]
    s_prev = [s_ref[q] for q in qs]
    o = [_bdot_nt(q_hat[q], s_prev[q]) + o_loc[q] for q in qs]
    g_corr = [_bdot_tn(a_hat[q], bbar[q]) * bdf for q in qs]
    h_new = [_bdot_tn(jnp.concatenate([w_v[q], v_q[q]], axis=0),
                      jnp.concatenate([bbar[q], kbar[q]], axis=0)) * bdf for q in qs]
    s_corr = [_bdot(s_prev[q], g_corr[q]) for q in qs]
    for q in qs:
        s_ref[q] = s_prev[q] * jnp.exp(total[q]) + s_corr[q] + h_new[q]

    mean = [_seg_sum(o[q], ones_bd) * (1.0 / HD) for q in qs]
    bsum = [_seg_sum(r_q[q] * kmod[q] * rk_ref[:, ls[q]], ones_bd) for q in qs]
    d = [o[q] - mean[q] for q in qs]
    var = [_seg_sum(d[q] * d[q], ones_bd) * (1.0 / HD) for q in qs]
    for q in qs:
        on = d[q] * lax.rsqrt(var[q] + RW_LN_EPS) * lnw_ref[:, ls[q]] + lnb_ref[:, ls[q]]
        y_ref[:, ls[q]] = ((on + bsum[q] * v_q[q]) * g_s[:, ls[q]]).astype(y_ref.dtype)


def rwkv_prompt(u, n_seq, seq_len, mu, w0, w2, a0, a2, g2, k_k, k_a, r_k, ln_w, ln_b):
    nc, steps, u_map, y_map = _prompt_rows(n_seq, seq_len, u.shape[0])
    const = lambda shape: pl.BlockSpec(shape, lambda b, c: (0, 0))
    y, s = pl.pallas_call(
        _chunk_kernel(_rw_prompt_kernel, nc, 12),
        grid=(n_seq, steps),
        in_specs=[pl.BlockSpec((CH, RW_PAD), u_map),
                  const((1, RW_PAD)), const((1, BR)), const((128, BR)), const((1, BR)), const((128, BR)),
                  const((RW_LG, BR)), const((1, BR)), const((1, BR)), const((1, BR)), const((1, BR)),
                  const((1, BR))],
        out_specs=[pl.BlockSpec((CH, BR), y_map),
                   pl.BlockSpec((None, NQ, QW, QW), lambda b, c: (b, 0, 0, 0))],
        out_shape=[jax.ShapeDtypeStruct((u.shape[0], BR), BF16),
                   jax.ShapeDtypeStruct((n_seq, NQ, QW, QW), F32)],
        scratch_shapes=[pltpu.VMEM((8, RW_PAD), F32)] + [pltpu.VMEM((CH, BR), F32)] * 6,
        compiler_params=_cparams(("parallel", "arbitrary")),
        name="rwkv_prompt",
    )(u, mu, w0, w2, a0, a2, g2, k_k, k_a, r_k, ln_w, ln_b)
    return y, s


def _rw_sample_pre_kernel(u_ref, sh_ref, mu_ref, w0_ref, w2_ref, a0_ref, a2_ref, g2_ref,
                          r_o, k_o, v_o, w_o, a_o, g_o):
    def xs_of(c0, width):
        cur = u_ref[:, c0:c0 + width]
        return cur + (sh_ref[:, c0:c0 + width] - cur) * mu_ref[:, c0:c0 + width]

    r, k, v, log_w, a, g = _rw_mix_inputs(xs_of, w0_ref, w2_ref, a0_ref, a2_ref, g2_ref)
    r_o[...] = r
    k_o[...] = k
    v_o[...] = v
    w_o[...] = jnp.exp(log_w)
    a_o[...] = a
    g_o[...] = g


def _rw_sample_step_kernel(r_ref, k_ref, v_ref, w_ref, a_ref, g_ref, kk_ref, ka_ref, rk_ref,
                           lnw_ref, lnb_ref, s_ref, y_ref, so_ref, o_scr):
    for h in range(r_ref.shape[0]):
        r, k, a = r_ref[h], k_ref[h], a_ref[h]
        w = w_ref[h]
        kk = k * kk_ref[h]
        kk = kk / jnp.maximum(jnp.sqrt(jnp.sum(kk * kk, axis=0, keepdims=True)), 1e-12)
        kmod = k * (1.0 + (a - 1.0) * ka_ref[h])
        av = -kk
        bv = kk * a

        def body(vi, carry):
            s = s_ref[h, vi]
            sa = jnp.sum(s * av, axis=0, keepdims=True)
            s_new = s * w + sa * bv + v_ref[h, pl.ds(vi, 1), :] * kmod
            so_ref[h, vi] = s_new
            o_scr[pl.ds(vi, 1), :] = jnp.sum(s_new * r, axis=0, keepdims=True)
            return carry

        lax.fori_loop(0, HD, body, 0)
        o = o_scr[...]
        mean = jnp.mean(o, axis=0, keepdims=True)
        d = o - mean
        var = jnp.mean(d * d, axis=0, keepdims=True)
        on = d * lax.rsqrt(var + RW_LN_EPS) * lnw_ref[h] + lnb_ref[h]
        bonus = jnp.sum(r * kmod * rk_ref[h], axis=0, keepdims=True) * v_ref[h]
        y_ref[h] = (on + bonus) * g_ref[h]


def rwkv_sample(u, row0, n_tok, state_t, shift_prev, layer, prev, mu, w0, w2, a0, a2, g2, k_k, k_a, r_k, ln_w, ln_b):
    const = lambda shape: pl.BlockSpec(shape, lambda i: (0, 0))
    vec = jax.ShapeDtypeStruct((n_tok, BR), F32)
    outs = pl.pallas_call(
        _rw_sample_pre_kernel,
        grid=(1,),
        in_specs=[pl.BlockSpec((n_tok, RW_PAD), lambda i: (row0 // n_tok, 0)),
                  pl.BlockSpec((None, n_tok, RW_PAD), lambda i: (layer, 0, 0)),
                  const((1, RW_PAD)), const((1, BR)), const((128, BR)), const((1, BR)), const((128, BR)),
                  const((RW_LG, BR))],
        out_specs=[pl.BlockSpec((n_tok, BR), lambda i: (0, 0))] * 6,
        out_shape=[vec] * 6,
        compiler_params=_cparams(("arbitrary",)),
        name="rwkv_sample_pre",
    )(u, shift_prev, mu, w0, w2, a0, a2, g2)
    outs = lax.optimization_barrier(outs)
    hb = 2
    vspec = pl.BlockSpec((hb, HD, n_tok), lambda i: (i, 0, 0))
    pspec = pl.BlockSpec((hb, HD, 1), lambda i: (i, 0, 0))
    kern, xspec, xarg, alias, lead = _state_chain(_rw_sample_step_kernel, 12, state_t.shape[0], prev)
    y, s = pl.pallas_call(
        kern,
        grid=(NH // hb,),
        input_output_aliases=alias,
        in_specs=[vspec] * 6 + [pspec] * 5
        + [pl.BlockSpec((None, hb, HD, HD, n_tok), lambda i: (layer, i, 0, 0, 0))] + xspec,
        out_specs=[vspec, pl.BlockSpec((lead, hb, HD, HD, n_tok), lambda i: (0 if lead else layer, i, 0, 0, 0))],
        out_shape=[jax.ShapeDtypeStruct((NH, HD, n_tok), F32),
                   jax.ShapeDtypeStruct(state_t.shape, F32)],
        scratch_shapes=[pltpu.VMEM((HD, n_tok), F32)],
        compiler_params=_cparams(("parallel",)),
        name="rwkv_sample_step",
    )(*[t.T.reshape(NH, HD, n_tok) for t in outs],
      *[t.reshape(NH, HD, 1) for t in (k_k, k_a, r_k, ln_w, ln_b)], state_t, *xarg)
    return y.reshape(BR, n_tok).T, s


def _pad_last(t, width):
    return jnp.pad(t, [(0, 0)] * (t.ndim - 1) + [(0, width - t.shape[-1])])


def _rw_to_padded(t):
    r_k_v = t[..., :3 * BR]
    wl = t[..., 3 * BR:3 * BR + RW_LW]
    al = t[..., 3 * BR + RW_LW:3 * BR + RW_LW + RW_LA]
    gl = t[..., 3 * BR + RW_LW + RW_LA:]
    return jnp.concatenate([r_k_v, _pad_last(wl, 128), _pad_last(al, 128), gl], axis=-1)


def _rw_from_padded(t):
    return jnp.concatenate([t[..., :RW_WL], t[..., RW_WL:RW_WL + RW_LW], t[..., RW_AL:RW_AL + RW_LA],
                            t[..., RW_GL:RW_GL + RW_LG]], axis=-1)


def _split_w_in(w_in):
    o1 = RW_COLS
    o2 = o1 + SS_COLS
    o3 = o2 + GL_COLS
    w = w_in.astype(BF16)
    return (_rw_to_padded(w[:, :, :o1]), _pad_last(w[:, :, o1:o2], SS_PAD), _pad_last(w[:, :, o2:o3], GL_PAD),
            w[:, :, o3:])


def _row(v):
    return v.reshape(1, -1).astype(F32)


def kernel(x_prompt, x_sample, p_prompt, p_sample, state_rwkv, state_shift, state_ssm, state_conv, state_gla, norm_mix, w_in, rw_mu, rw_w0, rw_w2, rw_a0, rw_a2, rw_g2, rw_kk, rw_ka, rw_rk, rw_ln_w, rw_ln_b, ssm_conv_w, ssm_conv_b, ssm_dt_bias, ssm_a_log, ssm_d, ssm_norm, gla_f_up, gla_f_bias, gla_norm, w_branch, w_out, norm_ffn, w_ff1, w_ff2, norm_ple, w_ple_gate, w_ple_proj, norm_final):
    depth = w_in.shape[0]
    bp, seq, d = x_prompt.shape
    ns = x_sample.shape[0]
    assert x_sample.shape[1] == 1, "the sample group is decoded one token at a time"
    assert seq % CH == 0 and seq >= 8
    tp = bp * seq
    assert tp % ns == 0 and ns % 8 == 0

    x = jnp.concatenate([x_prompt.reshape(tp, d), x_sample.reshape(ns, d)], axis=0)
    ple = jnp.concatenate([p_prompt.reshape(depth, tp, -1), p_sample.reshape(depth, ns, -1)], axis=1).astype(BF16)
    shift_pad = _rw_to_padded(state_shift)
    rw_state_t = jnp.transpose(state_rwkv, (0, 2, 3, 4, 1))
    ssm_state = state_ssm.reshape(depth, ns, SSM_G, BR // SSM_G, SSM_N)
    conv_t = jnp.transpose(state_conv, (0, 2, 1, 3))
    wb_branch, wb_ff2 = w_branch.astype(BF16), w_ff2.astype(BF16)

    outs = {k: [] for k in ("rw_p", "sh_p", "sh_s", "ss_p", "cv_p", "cv_s", "gl_p")}
    s_rw = s_ss = s_gl = None
    h = rms_norm(x, norm_mix[0], BF16)
    wb_rw, wb_ss, wb_gl, wb_gates = _split_w_in(w_in)
    for i in range(depth):
        u_rw = project(h, wb_rw, i)
        u_ss = project(h, wb_ss, i)
        u_gl = project(h, wb_gl, i)
        u_gate = project(h, wb_gates, i)

        rw_par = (_rw_to_padded(_row(rw_mu[i])), _row(rw_w0[i]),
                  jnp.pad(rw_w2[i], ((0, 128 - RW_LW), (0, 0))).astype(BF16), _row(rw_a0[i]),
                  jnp.pad(rw_a2[i], ((0, 128 - RW_LA), (0, 0))).astype(BF16), rw_g2[i].astype(BF16),
                  _row(rw_kk[i]), _row(rw_ka[i]), _row(rw_rk[i]), _row(rw_ln_w[i]), _row(rw_ln_b[i]))
        y_rw_p, s_bd = rwkv_prompt(u_rw, bp, seq, *rw_par)
        y_rw_s, s_rw = rwkv_sample(u_rw, tp, ns, rw_state_t, shift_pad, i, s_rw, *rw_par)
        sb = s_bd.reshape(bp, NQ, 4, HD, 4, HD)
        outs["rw_p"].append(jnp.stack([sb[:, :, j, :, j, :] for j in range(4)], axis=2).reshape(bp, NH, HD, HD))
        outs["sh_p"].append(_rw_from_padded(u_rw[seq - 1:tp:seq]))
        outs["sh_s"].append(_rw_from_padded(u_rw[tp:]))

        pad128 = lambda v: jnp.pad(v.astype(F32), (0, 128 - v.shape[0])).reshape(1, 128)
        ss_par = (ssm_conv_w[i], _row(ssm_conv_b[i]), pad128(ssm_dt_bias[i]),
                  pad128(-jnp.exp(ssm_a_log[i].astype(F32))), _row(jnp.repeat(ssm_d[i], HD)), _row(ssm_norm[i]))
        y_ss_p, h_p = ssm_prompt(u_ss, bp, seq, *ss_par)
        y_ss_s, s_ss = ssm_sample(u_ss, tp, ns, ssm_state, conv_t[i], i, s_ss, *ss_par)
        outs["ss_p"].append(h_p.reshape(bp, NH, HD, SSM_N))
        outs["cv_p"].append(jnp.stack([u_ss[(b + 1) * seq - (SSM_CONV - 1):(b + 1) * seq, SS_X:SS_X + SSM_XBC]
                                       for b in range(bp)]))
        outs["cv_s"].append(jnp.concatenate([state_conv[i][:, 1:], u_ss[tp:, None, SS_X:SS_X + SSM_XBC]], axis=1))

        gl_par = (jnp.pad(gla_f_up[i], ((0, 128 - GLA_LORA), (0, 0))).astype(BF16), _row(gla_f_bias[i]),
                  _row(gla_norm[i]))
        y_gl_p, g_p = gla_prompt(u_gl, bp, seq, *gl_par)
        y_gl_s, s_gl = gla_sample(u_gl, tp, ns, state_gla, i, s_gl, *gl_par)
        outs["gl_p"].append(g_p)

        ys = [insert_rows(yp, ysm, tp) for yp, ysm in ((y_rw_p, y_rw_s), (y_ss_p, y_ss_s), (y_gl_p, y_gl_s))]
        merged = merge_branches(ys, wb_branch, i, u_gate)
        x = residual_matmul(merged, w_out, i, x)
        hf = rms_norm(x, norm_ffn[i], BF16)
        x = ffn_down(ffn_up(hf, w_ff1, i), wb_ff2, i, x)
        hn = rms_norm(x, norm_ple[i], BF16)
        x = ple_update(ple, w_ple_proj, hn, w_ple_gate, i, x)
        if i + 1 < depth:
            h = rms_norm(x, norm_mix[i + 1], BF16)

    y_p, y_s = rms_norm_split(x, norm_final, tp, ns)
    st = lambda k: jnp.stack(outs[k])
    return (y_p.reshape(bp, seq, d), y_s.reshape(ns, 1, d), st("rw_p"),
            jnp.transpose(s_rw, (0, 4, 1, 2, 3)), st("sh_p"), st("sh_s"),
            st("ss_p"), s_ss.reshape(depth, ns, NH, HD, SSM_N), st("cv_p"), st("cv_s"), st("gl_p"), s_gl)
```

```python
import functools
import math

import jax
import jax.numpy as jnp
from jax import lax
from jax.experimental import pallas as pl
from jax.experimental.pallas import tpu as pltpu

F32 = jnp.float32
BF16 = jnp.bfloat16

BR = 2048
HD = 64
NH = BR // HD
QW = 256
NQ = BR // QW
CH = 64
RW_LW, RW_LA, RW_LG = 96, 96, 256
RW_LN_EPS = 64e-5
SSM_G, SSM_N, SSM_CONV = 4, 128, 4
SSM_XBC = BR + 2 * SSM_G * SSM_N
GLA_H, GLA_DK, GLA_DV, GLA_LORA, GLA_TAU = 4, 256, 512, 16, 16.0
GLA_K = GLA_H * GLA_DK
EPS = 1e-6

RW_R, RW_K, RW_V, RW_WL, RW_AL, RW_GL, RW_PAD = 0, 2048, 4096, 6144, 6272, 6400, 6656
RW_COLS = 3 * BR + RW_LW + RW_LA + RW_LG
SS_Z, SS_X, SS_B, SS_C, SS_DT, SS_PAD = 0, 2048, 4096, 4608, 5120, 5376
SS_COLS = BR + SSM_XBC + NH
GL_Q, GL_K, GL_V, GL_G, GL_F, GL_PAD = 0, 1024, 2048, 4096, 6144, 6400
GL_COLS = 2 * GLA_K + 2 * BR + GLA_LORA

V7X_VMEM_BYTES = 64 * 1024 * 1024
VMEM_REQUEST = 56 * 1024 * 1024
HI = lax.Precision.HIGHEST


def _cparams(sem):
    return pltpu.CompilerParams(dimension_semantics=sem, vmem_limit_bytes=VMEM_REQUEST)


def _row_tile(n, cap, align=16):
    best = None
    for t in range(align, min(n, cap) + 1, align):
        if n % t == 0:
            best = t
    return best if best is not None else n


def _col_tile(n, cap=1280):
    best = None
    for t in range(256, min(n, cap) + 1, 256):
        if n % t == 0:
            best = t
    if best is None:
        for t in range(128, min(n, cap) + 1, 128):
            if n % t == 0:
                best = t
    return best if best is not None else n


def _bdot(a, b):
    return jnp.dot(a.astype(BF16), b.astype(BF16), preferred_element_type=F32)


def _bdot_nt(a, b):
    return lax.dot_general(a.astype(BF16), b.astype(BF16), (((1,), (1,)), ((), ())),
                           preferred_element_type=F32)


def _bdot_tn(a, b):
    return lax.dot_general(a.astype(BF16), b.astype(BF16), (((0,), (0,)), ((), ())),
                           preferred_element_type=F32)


def _hdot(a, b):
    return jnp.dot(a, b, precision=HI, preferred_element_type=F32)


def _hdot_tn(a, b):
    return lax.dot_general(a, b, (((0,), (0,)), ((), ())), precision=HI,
                           preferred_element_type=F32)


def _split3(x):
    p1 = x.astype(BF16)
    r1 = x - p1.astype(F32)
    p2 = r1.astype(BF16)
    p3 = (r1 - p2.astype(F32)).astype(BF16)
    return p1, p2, p3


def _xdot(a, sel):
    sel = sel.astype(BF16)
    p1, p2, p3 = _split3(a)
    return (jnp.dot(p1, sel, preferred_element_type=F32) + jnp.dot(p2, sel, preferred_element_type=F32)
            + jnp.dot(p3, sel, preferred_element_type=F32))


def _xdot_l(sel, b):
    sel = sel.astype(BF16)
    p1, p2, p3 = _split3(b)
    return (jnp.dot(sel, p1, preferred_element_type=F32) + jnp.dot(sel, p2, preferred_element_type=F32)
            + jnp.dot(sel, p3, preferred_element_type=F32))


def _xdot_tn(a, sel):
    sel = sel.astype(BF16)
    dn = (((0,), (0,)), ((), ()))
    p1, p2, p3 = _split3(a)
    return (lax.dot_general(p1, sel, dn, preferred_element_type=F32)
            + lax.dot_general(p2, sel, dn, preferred_element_type=F32)
            + lax.dot_general(p3, sel, dn, preferred_element_type=F32))


def _sigmoid(x):
    return 1.0 / (1.0 + jnp.exp(-x))


def _silu(x):
    return x * _sigmoid(x)


def _softplus(x):
    return jnp.maximum(x, 0.0) + jnp.log(1.0 + jnp.exp(-jnp.abs(x)))


def _log_sigmoid(x):
    return -_softplus(-x)


def _tri_incl(n):
    r = lax.broadcasted_iota(jnp.int32, (n, n), 0)
    c = lax.broadcasted_iota(jnp.int32, (n, n), 1)
    return (c <= r).astype(F32)


def _rms_kernel(x_ref, g_ref, o_ref):
    x = x_ref[...]
    ms = jnp.mean(x * x, axis=-1, keepdims=True)
    o_ref[...] = (x * lax.rsqrt(ms + EPS) * g_ref[...]).astype(o_ref.dtype)


def rms_norm(x, g, out_dtype):
    t, d = x.shape
    tm = _row_tile(t, 520)
    return pl.pallas_call(
        _rms_kernel,
        grid=(t // tm,),
        in_specs=[pl.BlockSpec((tm, d), lambda i: (i, 0)),
                  pl.BlockSpec((1, d), lambda i: (0, 0))],
        out_specs=pl.BlockSpec((tm, d), lambda i: (i, 0)),
        out_shape=jax.ShapeDtypeStruct((t, d), out_dtype),
        compiler_params=_cparams(("parallel",)),
        name="rms_norm",
    )(x, g.reshape(1, d))


def _rms_split_kernel(x_ref, g_ref, head_ref, tail_ref, *, n_head):
    x = x_ref[...]
    ms = jnp.mean(x * x, axis=-1, keepdims=True)
    y = x * lax.rsqrt(ms + EPS) * g_ref[...]
    i = pl.program_id(0)

    @pl.when(i < n_head)
    def _():
        head_ref[...] = y

    @pl.when(i >= n_head)
    def _():
        tail_ref[...] = y


def rms_norm_split(x, g, n_first, tm):
    t, d = x.shape
    nh = n_first // tm
    return pl.pallas_call(
        functools.partial(_rms_split_kernel, n_head=nh),
        grid=(t // tm,),
        in_specs=[pl.BlockSpec((tm, d), lambda i: (i, 0)),
                  pl.BlockSpec((1, d), lambda i: (0, 0))],
        out_specs=[pl.BlockSpec((tm, d), lambda i: (jnp.minimum(i, nh - 1), 0)),
                   pl.BlockSpec((tm, d), lambda i: (jnp.maximum(i - nh, 0), 0))],
        out_shape=[jax.ShapeDtypeStruct((n_first, d), F32), jax.ShapeDtypeStruct((t - n_first, d), F32)],
        compiler_params=_cparams(("arbitrary",)),
        name="rms_norm_split",
    )(x, g.reshape(1, d))


def _proj_kernel(x_ref, w_ref, o_ref):
    o_ref[...] = jnp.dot(x_ref[...], w_ref[...], preferred_element_type=F32).astype(o_ref.dtype)


def project(x, w, layer, out_dtype=F32):
    t, k = x.shape
    n = w.shape[2]
    tm, tn = _row_tile(t, 1040), _col_tile(n)
    return pl.pallas_call(
        _proj_kernel,
        grid=(t // tm, n // tn),
        in_specs=[pl.BlockSpec((tm, k), lambda i, j: (i, 0)),
                  pl.BlockSpec((None, k, tn), lambda i, j: (layer, 0, j))],
        out_specs=pl.BlockSpec((tm, tn), lambda i, j: (i, j)),
        out_shape=jax.ShapeDtypeStruct((t, n), out_dtype),
        compiler_params=_cparams(("parallel", "arbitrary")),
        name="project",
    )(x, w)


def _merge_kernel(o1, o2, o3, w1, w2, w3, g1, g2, g3, out):
    acc = _sigmoid(g1[...]) * jnp.dot(o1[...], w1[...], preferred_element_type=F32)
    acc += _sigmoid(g2[...]) * jnp.dot(o2[...], w2[...], preferred_element_type=F32)
    acc += _sigmoid(g3[...]) * jnp.dot(o3[...], w3[...], preferred_element_type=F32)
    out[...] = acc.astype(out.dtype)


def merge_branches(ys, w_branch, layer, u_gate):
    t, kb = ys[0].shape
    d = w_branch.shape[3]
    tm, tn = _row_tile(t, 1040), _col_tile(d, 512)
    nb = d // tn
    y_spec = pl.BlockSpec((tm, kb), lambda i, j: (i, 0), pipeline_mode=pl.Buffered(1))
    w_specs = [pl.BlockSpec((None, None, kb, tn), functools.partial(lambda i, j, b: (layer, b, 0, j), b=b))
               for b in range(3)]
    g_specs = [pl.BlockSpec((tm, tn), functools.partial(lambda i, j, b: (i, b * nb + j), b=b))
               for b in range(3)]
    return pl.pallas_call(
        _merge_kernel,
        grid=(t // tm, nb),
        in_specs=[y_spec, y_spec, y_spec] + w_specs + g_specs,
        out_specs=pl.BlockSpec((tm, tn), lambda i, j: (i, j)),
        out_shape=jax.ShapeDtypeStruct((t, d), BF16),
        compiler_params=_cparams(("parallel", "arbitrary")),
        name="merge_branches",
    )(ys[0], ys[1], ys[2], w_branch, w_branch, w_branch, u_gate, u_gate, u_gate)


def _resid_kernel(a_ref, w_ref, x_ref, o_ref):
    o_ref[...] = x_ref[...] + jnp.dot(a_ref[...], w_ref[...].astype(BF16), preferred_element_type=F32)


def residual_matmul(a, w, layer, x):
    t, k = a.shape
    n = w.shape[2]
    tm, tn = _row_tile(t, 1040), _col_tile(n, 512)
    return pl.pallas_call(
        _resid_kernel,
        grid=(n // tn, t // tm),
        in_specs=[pl.BlockSpec((tm, k), lambda j, i: (i, 0)),
                  pl.BlockSpec((None, k, tn), lambda j, i: (layer, 0, j)),
                  pl.BlockSpec((tm, tn), lambda j, i: (i, j))],
        out_specs=pl.BlockSpec((tm, tn), lambda j, i: (i, j)),
        out_shape=jax.ShapeDtypeStruct((t, n), F32),
        compiler_params=_cparams(("parallel", "arbitrary")),
        name="residual_matmul",
    )(a, w, x)


def _ff1_kernel(a_ref, w_ref, o_ref):
    h = jnp.maximum(jnp.dot(a_ref[...], w_ref[...].astype(BF16), preferred_element_type=F32), 0.0)
    o_ref[...] = (h * h).astype(o_ref.dtype)


def ffn_up(a, w, layer):
    t, k = a.shape
    n = w.shape[2]
    tm, tn = _row_tile(t, 1040), _col_tile(n, 512)
    return pl.pallas_call(
        _ff1_kernel,
        grid=(n // tn, t // tm),
        in_specs=[pl.BlockSpec((tm, k), lambda j, i: (i, 0)),
                  pl.BlockSpec((None, k, tn), lambda j, i: (layer, 0, j))],
        out_specs=pl.BlockSpec((tm, tn), lambda j, i: (i, j)),
        out_shape=jax.ShapeDtypeStruct((t, n), BF16),
        compiler_params=_cparams(("parallel", "arbitrary")),
        name="ffn_up",
    )(a, w)


def _ff2_kernel(a_ref, w_ref, x_ref, o_ref, acc_ref):
    kk = pl.program_id(2)

    @pl.when(kk == 0)
    def _():
        acc_ref[...] = x_ref[...]

    acc_ref[...] += jnp.dot(a_ref[...], w_ref[...], preferred_element_type=F32)

    @pl.when(kk == pl.num_programs(2) - 1)
    def _():
        o_ref[...] = acc_ref[...]


def ffn_down(a, w, layer, x):
    t, k = a.shape
    n = w.shape[2]
    tm, tn, tk = _row_tile(t, 1040), _col_tile(n, 1024), _col_tile(k, 2048)
    return pl.pallas_call(
        _ff2_kernel,
        grid=(t // tm, n // tn, k // tk),
        in_specs=[pl.BlockSpec((tm, tk), lambda i, j, l: (i, l)),
                  pl.BlockSpec((None, tk, tn), lambda i, j, l: (layer, l, j)),
                  pl.BlockSpec((tm, tn), lambda i, j, l: (i, j))],
        out_specs=pl.BlockSpec((tm, tn), lambda i, j, l: (i, j)),
        out_shape=jax.ShapeDtypeStruct((t, n), F32),
        scratch_shapes=[pltpu.VMEM((tm, tn), F32)],
        compiler_params=_cparams(("parallel", "arbitrary", "arbitrary")),
        name="ffn_down",
    )(a, w, x)


def _ple_kernel(p_ref, wp_ref, h_ref, wg_ref, x_ref, o_ref):
    proj = jnp.dot(p_ref[...], wp_ref[...].astype(BF16), preferred_element_type=F32)
    gate = _sigmoid(jnp.dot(h_ref[...], wg_ref[...].astype(BF16), preferred_element_type=F32))
    o_ref[...] = x_ref[...] + proj * gate


def ple_update(p, wp, hn, wg, layer, x):
    t, kp = p.shape[1:]
    k = hn.shape[1]
    n = wg.shape[2]
    tm, tn = _row_tile(t, 1040), _col_tile(n, 512)
    return pl.pallas_call(
        _ple_kernel,
        grid=(n // tn, t // tm),
        in_specs=[pl.BlockSpec((None, tm, kp), lambda j, i: (layer, i, 0)),
                  pl.BlockSpec((None, kp, tn), lambda j, i: (layer, 0, j)),
                  pl.BlockSpec((tm, k), lambda j, i: (i, 0)),
                  pl.BlockSpec((None, k, tn), lambda j, i: (layer, 0, j)),
                  pl.BlockSpec((tm, tn), lambda j, i: (i, j))],
        out_specs=pl.BlockSpec((tm, tn), lambda j, i: (i, j)),
        out_shape=jax.ShapeDtypeStruct((t, n), F32),
        compiler_params=_cparams(("parallel", "arbitrary")),
        name="ple_update",
    )(p, wp, hn, wg, x)


def _gla_gates(fl, fup_ref, fb_ref):
    return _log_sigmoid(_bdot(fl, fup_ref[...]) + fb_ref[...]) * (1.0 / GLA_TAU)


def _gla_finish(o, g, nw_ref):
    ms = jnp.mean(o * o, axis=-1, keepdims=True)
    return o * lax.rsqrt(ms + EPS) * nw_ref[...] * _silu(g)


def _prompt_rows(n_seq, seq_len, n_rows):
    nc = seq_len // CH
    tail = n_rows - n_seq * seq_len
    assert tail % CH == 0 and tail // CH <= n_seq
    nt = tail // CH
    steps = nc + (1 if nt else 0)
    u_map = lambda b, c: (b * nc + jnp.minimum(c, nc - 1), 0)
    y_map = lambda b, c: (jnp.where(c < nc, b * nc + c, jnp.where(b < nt, n_seq * nc + b, b * nc + nc - 1)), 0)
    return (nc, nt), steps, u_map, y_map


def _chunk_kernel(body, nc_nt, y_index):
    nc, nt = nc_nt

    def kern(*refs):
        b = pl.program_id(0)
        c = pl.program_id(1)

        @pl.when(c < nc)
        def _():
            body(c, *refs)

        @pl.when(jnp.logical_and(c >= nc, b < nt))
        def _():
            refs[y_index][...] = jnp.zeros_like(refs[y_index])

    return kern


def _insert_kernel(rows_ref, buf_ref, o_ref):
    o_ref[...] = rows_ref[...].astype(o_ref.dtype)


def insert_rows(buf, rows, row0):
    n, w = rows.shape
    assert row0 % n == 0
    return pl.pallas_call(
        _insert_kernel,
        grid=(1,),
        in_specs=[pl.BlockSpec((n, w), lambda i: (0, 0)), pl.BlockSpec(memory_space=pl.ANY)],
        out_specs=pl.BlockSpec((n, w), lambda i: (row0 // n, 0)),
        out_shape=jax.ShapeDtypeStruct(buf.shape, buf.dtype),
        input_output_aliases={1: 0},
        compiler_params=_cparams(("arbitrary",)),
        name="insert_rows",
    )(rows, buf)


def _gla_prompt_kernel(c, u_ref, fup_ref, fb_ref, nw_ref, y_ref, s_ref):
    @pl.when(c == 0)
    def _():
        s_ref[...] = jnp.zeros_like(s_ref)

    lg = _gla_gates(u_ref[:, GL_F:GL_F + 128], fup_ref, fb_ref)
    tri = _tri_incl(CH)
    bcum = _xdot_l(tri, lg)
    ones_c = jnp.ones((CH, 128), F32)
    blast = bcum[CH - 1:CH, :]
    e_pos = jnp.exp(bcum)
    e_neg = jnp.exp(-bcum)
    e_end = jnp.exp(blast - bcum)
    for h in range(GLA_H):
        ks = slice(h * GLA_DK, (h + 1) * GLA_DK)
        vs = slice(h * GLA_DV, (h + 1) * GLA_DV)
        q = u_ref[:, GL_Q + h * GLA_DK:GL_Q + (h + 1) * GLA_DK] * (GLA_DK ** -0.5)
        k = u_ref[:, GL_K + h * GLA_DK:GL_K + (h + 1) * GLA_DK]
        v = u_ref[:, GL_V + h * GLA_DV:GL_V + (h + 1) * GLA_DV]
        g = u_ref[:, GL_G + h * GLA_DV:GL_G + (h + 1) * GLA_DV]
        q_in = q * e_pos[:, ks]
        k_in = k * e_neg[:, ks]
        k_end = k * e_end[:, ks]
        att = _bdot_nt(q_in, k_in) * tri
        s_prev = s_ref[h]
        o = _bdot(att, v) + _bdot(q_in, s_prev)
        dec = jnp.exp(_xdot_tn(lg[:, ks], ones_c))
        s_ref[h] = s_prev * jnp.concatenate([dec] * (GLA_DV // 128), axis=1) + _bdot_tn(k_end, v)
        y_ref[:, vs] = _gla_finish(o, g, nw_ref).astype(y_ref.dtype)


def gla_prompt(u, n_seq, seq_len, f_up, f_bias, norm_w):
    nc, steps, u_map, y_map = _prompt_rows(n_seq, seq_len, u.shape[0])
    y, s = pl.pallas_call(
        _chunk_kernel(_gla_prompt_kernel, nc, 4),
        grid=(n_seq, steps),
        in_specs=[pl.BlockSpec((CH, GL_PAD), u_map),
                  pl.BlockSpec((128, GLA_K), lambda b, c: (0, 0)),
                  pl.BlockSpec((1, GLA_K), lambda b, c: (0, 0)),
                  pl.BlockSpec((1, GLA_DV), lambda b, c: (0, 0))],
        out_specs=[pl.BlockSpec((CH, BR), y_map),
                   pl.BlockSpec((None, GLA_H, GLA_DK, GLA_DV), lambda b, c: (b, 0, 0, 0))],
        out_shape=[jax.ShapeDtypeStruct((u.shape[0], BR), BF16),
                   jax.ShapeDtypeStruct((n_seq, GLA_H, GLA_DK, GLA_DV), F32)],
        compiler_params=_cparams(("parallel", "arbitrary")),
        name="gla_prompt",
    )(u, f_up, f_bias, norm_w)
    return y, s


def _col_of_row(row, eye):
    return jnp.sum(eye * row, axis=-1, keepdims=True)


def _gla_sample_kernel(q_ref, k_ref, v_ref, g_ref, fl_ref, s_ref, fup_ref, fb_ref, nw_ref, y_ref, so_ref):
    nb = q_ref.shape[0]
    alpha = jnp.exp(_gla_gates(fl_ref[...], fup_ref, fb_ref))
    r = lax.broadcasted_iota(jnp.int32, (GLA_DK, GLA_DK), 0)
    c = lax.broadcasted_iota(jnp.int32, (GLA_DK, GLA_DK), 1)
    eye = (r == c).astype(F32)
    for b in range(nb):
        q_col = _col_of_row(q_ref[b:b + 1, :] * (GLA_DK ** -0.5), eye)
        k_col = _col_of_row(k_ref[b:b + 1, :], eye)
        a_col = _col_of_row(alpha[b:b + 1, :], eye)
        s_new = s_ref[b] * a_col + k_col * v_ref[b:b + 1, :]
        so_ref[b] = s_new
        o = jnp.sum(q_col * s_new, axis=0, keepdims=True)
        y_ref[b:b + 1, :] = _gla_finish(o, g_ref[b:b + 1, :], nw_ref)


def _state_chain(kernel_fn, n_in, depth, prev):
    if prev is None:
        def first(*refs):
            state_out = refs[n_in + 1]
            kernel_fn(*refs[:n_in + 1], state_out.at[0], *refs[n_in + 2:])
            for later in range(1, depth):
                state_out[later] = jnp.zeros(state_out.shape[1:], state_out.dtype)

        return first, [], [], {}, depth

    def chained(*refs):
        kernel_fn(*refs[:n_in], *refs[n_in + 1:])

    return chained, [pl.BlockSpec(memory_space=pl.ANY)], [prev], {n_in: 1}, None


def gla_sample(u, row0, n_tok, state, layer, prev, f_up, f_bias, norm_w):
    nb = 8
    r0 = row0 // nb
    col = lambda base, w: (lambda i, h: (r0 + i, base // w + h))
    kern, xspec, xarg, alias, lead = _state_chain(_gla_sample_kernel, 9, state.shape[0], prev)
    y, s = pl.pallas_call(
        kern,
        grid=(n_tok // nb, GLA_H),
        input_output_aliases=alias,
        in_specs=[pl.BlockSpec((nb, GLA_DK), col(GL_Q, GLA_DK)),
                  pl.BlockSpec((nb, GLA_DK), col(GL_K, GLA_DK)),
                  pl.BlockSpec((nb, GLA_DV), col(GL_V, GLA_DV)),
                  pl.BlockSpec((nb, GLA_DV), col(GL_G, GLA_DV)),
                  pl.BlockSpec((nb, 128), lambda i, h: (r0 + i, GL_F // 128)),
                  pl.BlockSpec((None, nb, None, GLA_DK, GLA_DV), lambda i, h: (layer, i, h, 0, 0)),
                  pl.BlockSpec((128, GLA_DK), lambda i, h: (0, h)),
                  pl.BlockSpec((1, GLA_DK), lambda i, h: (0, h)),
                  pl.BlockSpec((1, GLA_DV), lambda i, h: (0, 0))] + xspec,
        out_specs=[pl.BlockSpec((nb, GLA_DV), lambda i, h: (i, h)),
                   pl.BlockSpec((lead, nb, None, GLA_DK, GLA_DV),
                                lambda i, h: (0 if lead else layer, i, h, 0, 0))],
        out_shape=[jax.ShapeDtypeStruct((n_tok, BR), F32),
                   jax.ShapeDtypeStruct(state.shape, F32)],
        compiler_params=_cparams(("parallel", "parallel")),
        name="gla_sample",
    )(u, u, u, u, u, state, f_up, f_bias, norm_w, *xarg)
    return y, s


def _head_expand_matrix():
    r = lax.broadcasted_iota(jnp.int32, (128, BR), 0)
    c = lax.broadcasted_iota(jnp.int32, (128, BR), 1)
    return (r == c // HD).astype(F32)


def _stack4(x, mask4):
    return jnp.concatenate([x, x, x, x], axis=0) * mask4


def _quad_masks():
    r = lax.broadcasted_iota(jnp.int32, (4 * CH, QW), 0)
    c = lax.broadcasted_iota(jnp.int32, (4 * CH, QW), 1)
    return (r // CH == c // HD).astype(F32)


def _ssm_prompt_kernel(c, u_ref, cw_ref, cb_ref, dtb_ref, a_ref, d_ref, nw_ref, y_ref, h_ref,
                       carry_ref, x_s, dtl_s, cum_s, lal_s, yo_s):
    @pl.when(c == 0)
    def _():
        h_ref[...] = jnp.zeros_like(h_ref)
        carry_ref[...] = jnp.zeros_like(carry_ref)

    pre = u_ref[:, SS_X:SS_X + SSM_XBC]
    prev = carry_ref[...]
    row8 = lax.broadcasted_iota(jnp.int32, (8, SSM_XBC), 0)
    acc = cb_ref[...] + pre * cw_ref[SSM_CONV - 1:SSM_CONV, :]
    for j in range(1, SSM_CONV):
        sh = pltpu.roll(pre, j, axis=0)
        top = jnp.where(row8 < j, pltpu.roll(prev, j, axis=0), sh[0:8, :])
        sh = jnp.concatenate([top, sh[8:, :]], axis=0)
        acc = acc + sh * cw_ref[SSM_CONV - 1 - j:SSM_CONV - j, :]
    carry_ref[...] = pre[CH - 8:CH, :]
    xbc = _silu(acc)
    x_s[...] = xbc[:, 0:BR]

    expand = _head_expand_matrix()
    tri = _tri_incl(CH)
    dt = _softplus(u_ref[:, SS_DT:SS_DT + 128] + dtb_ref[...])
    la = dt * a_ref[...]
    dtl_s[...] = _xdot(dt, expand)
    lal_s[...] = _xdot(la, expand)
    cum_s[...] = _xdot(_xdot_l(tri, la), expand)

    mask4 = _quad_masks()
    t_i = lax.broadcasted_iota(jnp.int32, (CH, 4 * CH), 0)
    s_i = lax.broadcasted_iota(jnp.int32, (CH, 4 * CH), 1) % CH
    causal = s_i <= t_i
    upto = (t_i <= s_i).astype(F32)
    ones_cn = jnp.ones((CH, SSM_N), F32)
    for q in range(NQ):
        g = q // (NQ // SSM_G)
        ls = slice(q * QW, (q + 1) * QW)
        bm = xbc[:, BR + g * SSM_N:BR + (g + 1) * SSM_N]
        cm = xbc[:, BR + SSM_G * SSM_N + g * SSM_N:BR + SSM_G * SSM_N + (g + 1) * SSM_N]
        cum_q = cum_s[:, ls]
        lal_q = lal_s[:, ls]
        xdt = x_s[:, ls] * dtl_s[:, ls]
        cum_row = jnp.sum(lal_q * upto, axis=0, keepdims=True)
        lmat = jnp.exp(jnp.where(causal, cum_q - cum_row, -1e30))
        cb = _bdot_nt(cm, jnp.concatenate([bm, bm, bm, bm], axis=0))
        y_diag = _bdot(cb * lmat, _stack4(xdt, mask4))
        total = cum_q[CH - 1:CH, :]
        h_prev = h_ref[q * QW:(q + 1) * QW, :]
        y_off = _bdot_nt(cm, h_prev) * jnp.exp(cum_q)
        dec = jnp.exp(_xdot_tn(lal_q, ones_cn))
        h_ref[q * QW:(q + 1) * QW, :] = h_prev * dec + _bdot_tn(xdt * jnp.exp(total - cum_q), bm)
        yo_s[:, ls] = y_diag + y_off + d_ref[:, ls] * x_s[:, ls]

    gw = BR // SSM_G
    for g in range(SSM_G):
        gs = slice(g * gw, (g + 1) * gw)
        yg = yo_s[:, gs] * _silu(u_ref[:, SS_Z + g * gw:SS_Z + (g + 1) * gw])
        ms = jnp.mean(yg * yg, axis=-1, keepdims=True)
        y_ref[:, gs] = (yg * lax.rsqrt(ms + EPS) * nw_ref[:, gs]).astype(y_ref.dtype)


def ssm_prompt(u, n_seq, seq_len, conv_w, conv_b, dt_bias, a_neg, d_lane, norm_w):
    nc, steps, u_map, y_map = _prompt_rows(n_seq, seq_len, u.shape[0])
    const = lambda shape: pl.BlockSpec(shape, lambda b, c: (0, 0))
    y, h = pl.pallas_call(
        _chunk_kernel(_ssm_prompt_kernel, nc, 7),
        grid=(n_seq, steps),
        in_specs=[pl.BlockSpec((CH, SS_PAD), u_map),
                  const((SSM_CONV, SSM_XBC)), const((1, SSM_XBC)), const((1, 128)), const((1, 128)),
                  const((1, BR)), const((1, BR))],
        out_specs=[pl.BlockSpec((CH, BR), y_map),
                   pl.BlockSpec((None, BR, SSM_N), lambda b, c: (b, 0, 0))],
        out_shape=[jax.ShapeDtypeStruct((u.shape[0], BR), BF16),
                   jax.ShapeDtypeStruct((n_seq, BR, SSM_N), F32)],
        scratch_shapes=[pltpu.VMEM((8, SSM_XBC), F32)] + [pltpu.VMEM((CH, BR), F32)] * 5,
        compiler_params=_cparams(("parallel", "arbitrary")),
        name="ssm_prompt",
    )(u, conv_w, conv_b, dt_bias, a_neg, d_lane, norm_w)
    return y, h


def _ssm_sample_kernel(x_ref, b_ref, c_ref, dt_ref, z_ref, sx_ref, sb_ref, sc_ref,
                       cwx_ref, cwb_ref, cwc_ref, cbx_ref, cbb_ref, cbc_ref,
                       dtb_ref, a_ref, d_ref, nw_ref, h_ref, y_ref, ho_ref):
    nb = x_ref.shape[0]
    gw = x_ref.shape[1]

    def conv(cur_ref, st_ref, w_ref, bias_ref):
        acc = bias_ref[...] + cur_ref[...] * w_ref[SSM_CONV - 1:SSM_CONV, :]
        for j in range(SSM_CONV - 1):
            acc = acc + st_ref[j] * w_ref[j:j + 1, :]
        return _silu(acc)

    xs = conv(x_ref, sx_ref, cwx_ref, cbx_ref)
    bm = conv(b_ref, sb_ref, cwb_ref, cbb_ref)
    cm = conv(c_ref, sc_ref, cwc_ref, cbc_ref)
    g = pl.program_id(1)
    r = lax.broadcasted_iota(jnp.int32, (128, gw), 0)
    col = lax.broadcasted_iota(jnp.int32, (128, gw), 1)
    expand = (r == col // HD + g * (gw // HD)).astype(F32)
    dt = _softplus(dt_ref[...] + dtb_ref[...])
    dtl = _hdot(dt, expand)
    decl = jnp.exp(_hdot(dt * a_ref[...], expand))
    xdt = xs * dtl
    rr = lax.broadcasted_iota(jnp.int32, (gw, gw), 0)
    cc = lax.broadcasted_iota(jnp.int32, (gw, gw), 1)
    eye = (rr == cc).astype(F32)
    for b in range(nb):
        dec_col = _col_of_row(decl[b:b + 1, :], eye)
        xdt_col = _col_of_row(xdt[b:b + 1, :], eye)
        h_new = h_ref[b] * dec_col + xdt_col * bm[b:b + 1, :]
        ho_ref[b] = h_new
        y_col = jnp.sum(h_new * cm[b:b + 1, :], axis=-1, keepdims=True)
        y_row = jnp.sum(eye * y_col, axis=0, keepdims=True)
        y_row = (y_row + d_ref[...] * xs[b:b + 1, :]) * _silu(z_ref[b:b + 1, :])
        ms = jnp.mean(y_row * y_row, axis=-1, keepdims=True)
        y_ref[b:b + 1, :] = y_row * lax.rsqrt(ms + EPS) * nw_ref[...]


def ssm_sample(u, row0, n_tok, state, conv_state_t, layer, prev, conv_w, conv_b, dt_bias, a_neg, d_lane, norm_w):
    nb = 8
    kern, xspec, xarg, alias, lead = _state_chain(_ssm_sample_kernel, 19, state.shape[0], prev)
    r0 = row0 // nb
    gw = BR // SSM_G
    bb, cb = BR // SSM_N, (BR + SSM_G * SSM_N) // SSM_N
    ucol = lambda base, w: (lambda i, g: (r0 + i, base // w + g))
    scol = lambda base: (lambda i, g: (0, i, base + g))
    wcol = lambda base: (lambda i, g: (0, base + g))
    y, h = pl.pallas_call(
        kern,
        grid=(n_tok // nb, SSM_G),
        input_output_aliases=alias,
        in_specs=[pl.BlockSpec((nb, gw), ucol(SS_X, gw)),
                  pl.BlockSpec((nb, SSM_N), ucol(SS_B, SSM_N)),
                  pl.BlockSpec((nb, SSM_N), ucol(SS_C, SSM_N)),
                  pl.BlockSpec((nb, 128), lambda i, g: (r0 + i, SS_DT // 128)),
                  pl.BlockSpec((nb, gw), ucol(SS_Z, gw)),
                  pl.BlockSpec((SSM_CONV - 1, nb, gw), lambda i, g: (0, i, g)),
                  pl.BlockSpec((SSM_CONV - 1, nb, SSM_N), scol(bb)),
                  pl.BlockSpec((SSM_CONV - 1, nb, SSM_N), scol(cb)),
                  pl.BlockSpec((SSM_CONV, gw), lambda i, g: (0, g)),
                  pl.BlockSpec((SSM_CONV, SSM_N), wcol(bb)),
                  pl.BlockSpec((SSM_CONV, SSM_N), wcol(cb)),
                  pl.BlockSpec((1, gw), lambda i, g: (0, g)),
                  pl.BlockSpec((1, SSM_N), wcol(bb)),
                  pl.BlockSpec((1, SSM_N), wcol(cb)),
                  pl.BlockSpec((1, 128), lambda i, g: (0, 0)),
                  pl.BlockSpec((1, 128), lambda i, g: (0, 0)),
                  pl.BlockSpec((1, gw), lambda i, g: (0, g)),
                  pl.BlockSpec((1, gw), lambda i, g: (0, g)),
                  pl.BlockSpec((None, nb, None, gw, SSM_N), lambda i, g: (layer, i, g, 0, 0))] + xspec,
        out_specs=[pl.BlockSpec((nb, gw), lambda i, g: (i, g)),
                   pl.BlockSpec((lead, nb, None, gw, SSM_N), lambda i, g: (0 if lead else layer, i, g, 0, 0))],
        out_shape=[jax.ShapeDtypeStruct((n_tok, BR), F32),
                   jax.ShapeDtypeStruct(state.shape, F32)],
        compiler_params=_cparams(("parallel", "parallel")),
        name="ssm_sample",
    )(u, u, u, u, u, conv_state_t, conv_state_t, conv_state_t, conv_w, conv_w, conv_w,
      conv_b, conv_b, conv_b, dt_bias, a_neg, d_lane, norm_w, state, *xarg)
    return y, h


def _rw_mix_inputs(xs_of, w0_ref, w2_ref, a0_ref, a2_ref, g2_ref):
    r = xs_of(RW_R, BR)
    k = xs_of(RW_K, BR)
    v = xs_of(RW_V, BR)
    w = -_softplus(-(w0_ref[...] + _bdot(jnp.tanh(xs_of(RW_WL, 128)), w2_ref[...]))) - 0.5
    log_w = -jnp.exp(w)
    a = _sigmoid(a0_ref[...] + _bdot(xs_of(RW_AL, 128), a2_ref[...]))
    g = _bdot(_sigmoid(xs_of(RW_GL, RW_LG)), g2_ref[...])
    return r, k, v, log_w, a, g


def _seg_sum(x, ones_bd):
    hi = x.astype(BF16)
    lo = (x - hi.astype(F32)).astype(BF16)
    return (jnp.dot(hi, ones_bd, preferred_element_type=F32)
            + jnp.dot(lo, ones_bd, preferred_element_type=F32))


def _rw_prompt_kernel(c, u_ref, mu_ref, w0_ref, w2_ref, a0_ref, a2_ref, g2_ref, kk_ref, ka_ref, rk_ref,
                      lnw_ref, lnb_ref, y_ref, s_ref, carry_ref, r_s, k_s, v_s, lw_s, a_s, g_s):
    @pl.when(c == 0)
    def _():
        s_ref[...] = jnp.zeros_like(s_ref)
        carry_ref[...] = jnp.zeros_like(carry_ref)

    row8 = lax.broadcasted_iota(jnp.int32, (8, 1), 0)

    def xs_of(c0, width):
        cur = u_ref[:, c0:c0 + width]
        sh = pltpu.roll(cur, 1, axis=0)
        top = jnp.where(row8 < 1, pltpu.roll(carry_ref[:, c0:c0 + width], 1, axis=0), sh[0:8, :])
        prev = jnp.concatenate([top, sh[8:, :]], axis=0)
        return cur + (prev - cur) * mu_ref[:, c0:c0 + width]

    r, k, v, log_w, a, g = _rw_mix_inputs(xs_of, w0_ref, w2_ref, a0_ref, a2_ref, g2_ref)
    r_s[...] = r
    k_s[...] = k
    v_s[...] = v
    lw_s[...] = log_w
    a_s[...] = a
    g_s[...] = g
    carry_ref[...] = u_ref[CH - 8:CH, :]

    tri = _tri_incl(CH)
    mask4 = _quad_masks()
    rb = lax.broadcasted_iota(jnp.int32, (QW, QW), 0)
    cb = lax.broadcasted_iota(jnp.int32, (QW, QW), 1)
    bd = (rb // HD == cb // HD)
    bdf = bd.astype(F32)
    ones_bd = bd.astype(BF16)
    t_i = lax.broadcasted_iota(jnp.int32, (CH, 4 * CH), 0)
    s_i = lax.broadcasted_iota(jnp.int32, (CH, 4 * CH), 1) % CH
    strict = (s_i < t_i).astype(F32)
    incl = (s_i <= t_i).astype(F32)
    ident = (s_i == t_i).astype(F32)

    def blockdiag(x):
        return jnp.concatenate([x, x, x, x], axis=0) * bdf

    qs = range(NQ)
    ls = [slice(q * QW, (q + 1) * QW) for q in qs]
    stk = lambda x: _stack4(x, mask4)
    r_q = [r_s[:, ls[q]] for q in qs]
    v_q = [v_s[:, ls[q]] for q in qs]
    a_q = [a_s[:, ls[q]] for q in qs]
    kk = [k_s[:, ls[q]] * kk_ref[:, ls[q]] for q in qs]
    kn = [_seg_sum(kk[q] * kk[q], ones_bd) for q in qs]
    cum = [_xdot_l(tri, lw_s[:, ls[q]]) for q in qs]
    kk = [kk[q] / jnp.maximum(jnp.sqrt(kn[q]), 1e-12) for q in qs]
    kmod = [k_s[:, ls[q]] * (1.0 + (a_q[q] - 1.0) * ka_ref[:, ls[q]]) for q in qs]
    bv = [kk[q] * a_q[q] for q in qs]
    total = [cum[q][CH - 1:CH, :] for q in qs]
    e_neg = [jnp.exp(-cum[q]) for q in qs]
    e_end = [jnp.exp(total[q] - cum[q]) for q in qs]
    at = [-kk[q] * jnp.exp(cum[q] - lw_s[:, ls[q]]) for q in qs]
    rt = [r_q[q] * jnp.exp(cum[q]) for q in qs]
    bbar = [bv[q] * e_end[q] for q in qs]
    kbar = [kmod[q] * e_end[q] for q in qs]

    gram = [_bdot_nt(jnp.concatenate([at[q], rt[q]], axis=0),
                     jnp.concatenate([stk(bv[q] * e_neg[q]), stk(kmod[q] * e_neg[q])], axis=0))
            for q in qs]
    a_ab = [gram[q][0:CH, 0:4 * CH] * strict for q in qs]
    a_ak = [gram[q][0:CH, 4 * CH:8 * CH] * strict for q in qs]
    m_rbk = [jnp.concatenate([gram[q][CH:2 * CH, 0:4 * CH] * incl, gram[q][CH:2 * CH, 4 * CH:8 * CH] * incl],
                             axis=1) for q in qs]

    tinv = [ident + a_ab[q] for q in qs]
    p = a_ab
    for _ in range(5):
        p = [_bdot(p[q], blockdiag(p[q])) for q in qs]
        tinv = [tinv[q] + _bdot(tinv[q], blockdiag(p[q])) for q in qs]

    x1 = [_bdot(a_ak[q], stk(v_q[q])) for q in qs]
    a_hat = [_bdot(tinv[q], stk(at[q])) for q in qs]
    w_v = [_bdot(tinv[q], stk(x1[q])) for q in qs]
    q_hat = [rt[q] + _bdot(m_rbk[q][:, 0:4 * CH], stk(a_hat[q])) for q in qs]
    o_loc = [_bdot(m_rbk[q], jnp.concatenate([stk(w_v[q]), stk(v_q[q])], axis=0)) for q in qs]
    s_prev = [s_ref[q] for q in qs]
    o = [_bdot_nt(q_hat[q], s_prev[q]) + o_loc[q] for q in qs]
    g_corr = [_bdot_tn(a_hat[q], bbar[q]) * bdf for q in qs]
    h_new = [_bdot_tn(jnp.concatenate([w_v[q], v_q[q]], axis=0),
                      jnp.concatenate([bbar[q], kbar[q]], axis=0)) * bdf for q in qs]
    s_corr = [_bdot(s_prev[q], g_corr[q]) for q in qs]
    for q in qs:
        s_ref[q] = s_prev[q] * jnp.exp(total[q]) + s_corr[q] + h_new[q]

    mean = [_seg_sum(o[q], ones_bd) * (1.0 / HD) for q in qs]
    bsum = [_seg_sum(r_q[q] * kmod[q] * rk_ref[:, ls[q]], ones_bd) for q in qs]
    d = [o[q] - mean[q] for q in qs]
    var = [_seg_sum(d[q] * d[q], ones_bd) * (1.0 / HD) for q in qs]
    for q in qs:
        on = d[q] * lax.rsqrt(var[q] + RW_LN_EPS) * lnw_ref[:, ls[q]] + lnb_ref[:, ls[q]]
        y_ref[:, ls[q]] = ((on + bsum[q] * v_q[q]) * g_s[:, ls[q]]).astype(y_ref.dtype)


def rwkv_prompt(u, n_seq, seq_len, mu, w0, w2, a0, a2, g2, k_k, k_a, r_k, ln_w, ln_b):
    nc, steps, u_map, y_map = _prompt_rows(n_seq, seq_len, u.shape[0])
    const = lambda shape: pl.BlockSpec(shape, lambda b, c: (0, 0))
    y, s = pl.pallas_call(
        _chunk_kernel(_rw_prompt_kernel, nc, 12),
        grid=(n_seq, steps),
        in_specs=[pl.BlockSpec((CH, RW_PAD), u_map),
                  const((1, RW_PAD)), const((1, BR)), const((128, BR)), const((1, BR)), const((128, BR)),
                  const((RW_LG, BR)), const((1, BR)), const((1, BR)), const((1, BR)), const((1, BR)),
                  const((1, BR))],
        out_specs=[pl.BlockSpec((CH, BR), y_map),
                   pl.BlockSpec((None, NQ, QW, QW), lambda b, c: (b, 0, 0, 0))],
        out_shape=[jax.ShapeDtypeStruct((u.shape[0], BR), BF16),
                   jax.ShapeDtypeStruct((n_seq, NQ, QW, QW), F32)],
        scratch_shapes=[pltpu.VMEM((8, RW_PAD), F32)] + [pltpu.VMEM((CH, BR), F32)] * 6,
        compiler_params=_cparams(("parallel", "arbitrary")),
        name="rwkv_prompt",
    )(u, mu, w0, w2, a0, a2, g2, k_k, k_a, r_k, ln_w, ln_b)
    return y, s


def _rw_sample_pre_kernel(u_ref, sh_ref, mu_ref, w0_ref, w2_ref, a0_ref, a2_ref, g2_ref,
                          r_o, k_o, v_o, w_o, a_o, g_o):
    def xs_of(c0, width):
        cur = u_ref[:, c0:c0 + width]
        return cur + (sh_ref[:, c0:c0 + width] - cur) * mu_ref[:, c0:c0 + width]

    r, k, v, log_w, a, g = _rw_mix_inputs(xs_of, w0_ref, w2_ref, a0_ref, a2_ref, g2_ref)
    r_o[...] = r
    k_o[...] = k
    v_o[...] = v
    w_o[...] = jnp.exp(log_w)
    a_o[...] = a
    g_o[...] = g


def _rw_sample_step_kernel(r_ref, k_ref, v_ref, w_ref, a_ref, g_ref, kk_ref, ka_ref, rk_ref,
                           lnw_ref, lnb_ref, s_ref, y_ref, so_ref, o_scr):
    for h in range(r_ref.shape[0]):
        r, k, a = r_ref[h], k_ref[h], a_ref[h]
        w = w_ref[h]
        kk = k * kk_ref[h]
        kk = kk / jnp.maximum(jnp.sqrt(jnp.sum(kk * kk, axis=0, keepdims=True)), 1e-12)
        kmod = k * (1.0 + (a - 1.0) * ka_ref[h])
        av = -kk
        bv = kk * a

        def body(vi, carry):
            s = s_ref[h, vi]
            sa = jnp.sum(s * av, axis=0, keepdims=True)
            s_new = s * w + sa * bv + v_ref[h, pl.ds(vi, 1), :] * kmod
            so_ref[h, vi] = s_new
            o_scr[pl.ds(vi, 1), :] = jnp.sum(s_new * r, axis=0, keepdims=True)
            return carry

        lax.fori_loop(0, HD, body, 0)
        o = o_scr[...]
        mean = jnp.mean(o, axis=0, keepdims=True)
        d = o - mean
        var = jnp.mean(d * d, axis=0, keepdims=True)
        on = d * lax.rsqrt(var + RW_LN_EPS) * lnw_ref[h] + lnb_ref[h]
        bonus = jnp.sum(r * kmod * rk_ref[h], axis=0, keepdims=True) * v_ref[h]
        y_ref[h] = (on + bonus) * g_ref[h]


def rwkv_sample(u, row0, n_tok, state_t, shift_prev, layer, prev, mu, w0, w2, a0, a2, g2, k_k, k_a, r_k, ln_w, ln_b):
    const = lambda shape: pl.BlockSpec(shape, lambda i: (0, 0))
    vec = jax.ShapeDtypeStruct((n_tok, BR), F32)
    outs = pl.pallas_call(
        _rw_sample_pre_kernel,
        grid=(1,),
        in_specs=[pl.BlockSpec((n_tok, RW_PAD), lambda i: (row0 // n_tok, 0)),
                  pl.BlockSpec((None, n_tok, RW_PAD), lambda i: (layer, 0, 0)),
                  const((1, RW_PAD)), const((1, BR)), const((128, BR)), const((1, BR)), const((128, BR)),
                  const((RW_LG, BR))],
        out_specs=[pl.BlockSpec((n_tok, BR), lambda i: (0, 0))] * 6,
        out_shape=[vec] * 6,
        compiler_params=_cparams(("arbitrary",)),
        name="rwkv_sample_pre",
    )(u, shift_prev, mu, w0, w2, a0, a2, g2)
    outs = lax.optimization_barrier(outs)
    hb = 2
    vspec = pl.BlockSpec((hb, HD, n_tok), lambda i: (i, 0, 0))
    pspec = pl.BlockSpec((hb, HD, 1), lambda i: (i, 0, 0))
    kern, xspec, xarg, alias, lead = _state_chain(_rw_sample_step_kernel, 12, state_t.shape[0], prev)
    y, s = pl.pallas_call(
        kern,
        grid=(NH // hb,),
        input_output_aliases=alias,
        in_specs=[vspec] * 6 + [pspec] * 5
        + [pl.BlockSpec((None, hb, HD, HD, n_tok), lambda i: (layer, i, 0, 0, 0))] + xspec,
        out_specs=[vspec, pl.BlockSpec((lead, hb, HD, HD, n_tok), lambda i: (0 if lead else layer, i, 0, 0, 0))],
        out_shape=[jax.ShapeDtypeStruct((NH, HD, n_tok), F32),
                   jax.ShapeDtypeStruct(state_t.shape, F32)],
        scratch_shapes=[pltpu.VMEM((HD, n_tok), F32)],
        compiler_params=_cparams(("parallel",)),
        name="rwkv_sample_step",
    )(*[t.T.reshape(NH, HD, n_tok) for t in outs],
      *[t.reshape(NH, HD, 1) for t in (k_k, k_a, r_k, ln_w, ln_b)], state_t, *xarg)
    return y.reshape(BR, n_tok).T, s


def _pad_last(t, width):
    return jnp.pad(t, [(0, 0)] * (t.ndim - 1) + [(0, width - t.shape[-1])])


def _rw_to_padded(t):
    r_k_v = t[..., :3 * BR]
    wl = t[..., 3 * BR:3 * BR + RW_LW]
    al = t[..., 3 * BR + RW_LW:3 * BR + RW_LW + RW_LA]
    gl = t[..., 3 * BR + RW_LW + RW_LA:]
    return jnp.concatenate([r_k_v, _pad_last(wl, 128), _pad_last(al, 128), gl], axis=-1)


def _rw_from_padded(t):
    return jnp.concatenate([t[..., :RW_WL], t[..., RW_WL:RW_WL + RW_LW], t[..., RW_AL:RW_AL + RW_LA],
                            t[..., RW_GL:RW_GL + RW_LG]], axis=-1)


def _split_w_in(w_in):
    o1 = RW_COLS
    o2 = o1 + SS_COLS
    o3 = o2 + GL_COLS
    w = w_in.astype(BF16)
    return (_rw_to_padded(w[:, :, :o1]), _pad_last(w[:, :, o1:o2], SS_PAD), _pad_last(w[:, :, o2:o3], GL_PAD),
            w[:, :, o3:])


def _row(v):
    return v.reshape(1, -1).astype(F32)


def kernel(x_prompt, x_sample, p_prompt, p_sample, state_rwkv, state_shift, state_ssm, state_conv, state_gla, norm_mix, w_in, rw_mu, rw_w0, rw_w2, rw_a0, rw_a2, rw_g2, rw_kk, rw_ka, rw_rk, rw_ln_w, rw_ln_b, ssm_conv_w, ssm_conv_b, ssm_dt_bias, ssm_a_log, ssm_d, ssm_norm, gla_f_up, gla_f_bias, gla_norm, w_branch, w_out, norm_ffn, w_ff1, w_ff2, norm_ple, w_ple_gate, w_ple_proj, norm_final):
    depth = w_in.shape[0]
    bp, seq, d = x_prompt.shape
    ns = x_sample.shape[0]
    assert x_sample.shape[1] == 1, "the sample group is decoded one token at a time"
    assert seq % CH == 0 and seq >= 8
    tp = bp * seq
    assert tp % ns == 0 and ns % 8 == 0

    x = jnp.concatenate([x_prompt.reshape(tp, d), x_sample.reshape(ns, d)], axis=0)
    ple = jnp.concatenate([p_prompt.reshape(depth, tp, -1), p_sample.reshape(depth, ns, -1)], axis=1).astype(BF16)
    shift_pad = _rw_to_padded(state_shift)
    rw_state_t = jnp.transpose(state_rwkv, (0, 2, 3, 4, 1))
    ssm_state = state_ssm.reshape(depth, ns, SSM_G, BR // SSM_G, SSM_N)
    conv_t = jnp.transpose(state_conv, (0, 2, 1, 3))
    wb_branch, wb_ff2 = w_branch.astype(BF16), w_ff2.astype(BF16)

    outs = {k: [] for k in ("rw_p", "sh_p", "sh_s", "ss_p", "cv_p", "cv_s", "gl_p")}
    s_rw = s_ss = s_gl = None
    h = rms_norm(x, norm_mix[0], BF16)
    wb_rw, wb_ss, wb_gl, wb_gates = _split_w_in(w_in)
    for i in range(depth):
        u_rw = project(h, wb_rw, i)
        u_ss = project(h, wb_ss, i)
        u_gl = project(h, wb_gl, i)
        u_gate = project(h, wb_gates, i)

        rw_par = (_rw_to_padded(_row(rw_mu[i])), _row(rw_w0[i]),
                  jnp.pad(rw_w2[i], ((0, 128 - RW_LW), (0, 0))).astype(BF16), _row(rw_a0[i]),
                  jnp.pad(rw_a2[i], ((0, 128 - RW_LA), (0, 0))).astype(BF16), rw_g2[i].astype(BF16),
                  _row(rw_kk[i]), _row(rw_ka[i]), _row(rw_rk[i]), _row(rw_ln_w[i]), _row(rw_ln_b[i]))
        y_rw_p, s_bd = rwkv_prompt(u_rw, bp, seq, *rw_par)
        y_rw_s, s_rw = rwkv_sample(u_rw, tp, ns, rw_state_t, shift_pad, i, s_rw, *rw_par)
        sb = s_bd.reshape(bp, NQ, 4, HD, 4, HD)
        outs["rw_p"].append(jnp.stack([sb[:, :, j, :, j, :] for j in range(4)], axis=2).reshape(bp, NH, HD, HD))
        outs["sh_p"].append(_rw_from_padded(u_rw[seq - 1:tp:seq]))
        outs["sh_s"].append(_rw_from_padded(u_rw[tp:]))

        pad128 = lambda v: jnp.pad(v.astype(F32), (0, 128 - v.shape[0])).reshape(1, 128)
        ss_par = (ssm_conv_w[i], _row(ssm_conv_b[i]), pad128(ssm_dt_bias[i]),
                  pad128(-jnp.exp(ssm_a_log[i].astype(F32))), _row(jnp.repeat(ssm_d[i], HD)), _row(ssm_norm[i]))
        y_ss_p, h_p = ssm_prompt(u_ss, bp, seq, *ss_par)
        y_ss_s, s_ss = ssm_sample(u_ss, tp, ns, ssm_state, conv_t[i], i, s_ss, *ss_par)
        outs["ss_p"].append(h_p.reshape(bp, NH, HD, SSM_N))
        outs["cv_p"].append(jnp.stack([u_ss[(b + 1) * seq - (SSM_CONV - 1):(b + 1) * seq, SS_X:SS_X + SSM_XBC]
                                       for b in range(bp)]))
        outs["cv_s"].append(jnp.concatenate([state_conv[i][:, 1:], u_ss[tp:, None, SS_X:SS_X + SSM_XBC]], axis=1))

        gl_par = (jnp.pad(gla_f_up[i], ((0, 128 - GLA_LORA), (0, 0))).astype(BF16), _row(gla_f_bias[i]),
                  _row(gla_norm[i]))
        y_gl_p, g_p = gla_prompt(u_gl, bp, seq, *gl_par)
        y_gl_s, s_gl = gla_sample(u_gl, tp, ns, state_gla, i, s_gl, *gl_par)
        outs["gl_p"].append(g_p)

        ys = [insert_rows(yp, ysm, tp) for yp, ysm in ((y_rw_p, y_rw_s), (y_ss_p, y_ss_s), (y_gl_p, y_gl_s))]
        merged = merge_branches(ys, wb_branch, i, u_gate)
        x = residual_matmul(merged, w_out, i, x)
        hf = rms_norm(x, norm_ffn[i], BF16)
        x = ffn_down(ffn_up(hf, w_ff1, i), wb_ff2, i, x)
        hn = rms_norm(x, norm_ple[i], BF16)
        x = ple_update(ple, w_ple_proj, hn, w_ple_gate, i, x)
        if i + 1 < depth:
            h = rms_norm(x, norm_mix[i + 1], BF16)

    y_p, y_s = rms_norm_split(x, norm_final, tp, ns)
    st = lambda k: jnp.stack(outs[k])
    return (y_p.reshape(bp, seq, d), y_s.reshape(ns, 1, d), st("rw_p"),
            jnp.transpose(s_rw, (0, 4, 1, 2, 3)), st("sh_p"), st("sh_s"),
            st("ss_p"), s_ss.reshape(depth, ns, NH, HD, SSM_N), st("cv_p"), st("cv_s"), st("gl_p"), s_gl)
```

```python
import functools
import math

import jax
import jax.numpy as jnp
from jax import lax
from jax.experimental import pallas as pl
from jax.experimental.pallas import tpu as pltpu

F32 = jnp.float32
BF16 = jnp.bfloat16

BR = 2048
HD = 64
NH = BR // HD
QW = 256
NQ = BR // QW
CH = 64
RW_LW, RW_LA, RW_LG = 96, 96, 256
RW_LN_EPS = 64e-5
SSM_G, SSM_N, SSM_CONV = 4, 128, 4
SSM_XBC = BR + 2 * SSM_G * SSM_N
GLA_H, GLA_DK, GLA_DV, GLA_LORA, GLA_TAU = 4, 256, 512, 16, 16.0
GLA_K = GLA_H * GLA_DK
EPS = 1e-6

RW_R, RW_K, RW_V, RW_WL, RW_AL, RW_GL, RW_PAD = 0, 2048, 4096, 6144, 6272, 6400, 6656
RW_COLS = 3 * BR + RW_LW + RW_LA + RW_LG
SS_Z, SS_X, SS_B, SS_C, SS_DT, SS_PAD = 0, 2048, 4096, 4608, 5120, 5376
SS_COLS = BR + SSM_XBC + NH
GL_Q, GL_K, GL_V, GL_G, GL_F, GL_PAD = 0, 1024, 2048, 4096, 6144, 6400
GL_COLS = 2 * GLA_K + 2 * BR + GLA_LORA

V7X_VMEM_BYTES = 64 * 1024 * 1024
VMEM_REQUEST = 56 * 1024 * 1024
HI = lax.Precision.HIGHEST


def _cparams(sem):
    return pltpu.CompilerParams(dimension_semantics=sem, vmem_limit_bytes=VMEM_REQUEST)


def _row_tile(n, cap, align=16):
    best = None
    for t in range(align, min(n, cap) + 1, align):
        if n % t == 0:
            best = t
    return best if best is not None else n


def _col_tile(n, cap=1280):
    best = None
    for t in range(256, min(n, cap) + 1, 256):
        if n % t == 0:
            best = t
    if best is None:
        for t in range(128, min(n, cap) + 1, 128):
            if n % t == 0:
                best = t
    return best if best is not None else n


def _bdot(a, b):
    return jnp.dot(a.astype(BF16), b.astype(BF16), preferred_element_type=F32)


def _bdot_nt(a, b):
    return lax.dot_general(a.astype(BF16), b.astype(BF16), (((1,), (1,)), ((), ())),
                           preferred_element_type=F32)


def _bdot_tn(a, b):
    return lax.dot_general(a.astype(BF16), b.astype(BF16), (((0,), (0,)), ((), ())),
                           preferred_element_type=F32)


def _hdot(a, b):
    return jnp.dot(a, b, precision=HI, preferred_element_type=F32)


def _hdot_tn(a, b):
    return lax.dot_general(a, b, (((0,), (0,)), ((), ())), precision=HI,
                           preferred_element_type=F32)


def _split3(x):
    p1 = x.astype(BF16)
    r1 = x - p1.astype(F32)
    p2 = r1.astype(BF16)
    p3 = (r1 - p2.astype(F32)).astype(BF16)
    return p1, p2, p3


def _xdot(a, sel):
    sel = sel.astype(BF16)
    p1, p2, p3 = _split3(a)
    return (jnp.dot(p1, sel, preferred_element_type=F32) + jnp.dot(p2, sel, preferred_element_type=F32)
            + jnp.dot(p3, sel, preferred_element_type=F32))


def _xdot_l(sel, b):
    sel = sel.astype(BF16)
    p1, p2, p3 = _split3(b)
    return (jnp.dot(sel, p1, preferred_element_type=F32) + jnp.dot(sel, p2, preferred_element_type=F32)
            + jnp.dot(sel, p3, preferred_element_type=F32))


def _xdot_tn(a, sel):
    sel = sel.astype(BF16)
    dn = (((0,), (0,)), ((), ()))
    p1, p2, p3 = _split3(a)
    return (lax.dot_general(p1, sel, dn, preferred_element_type=F32)
            + lax.dot_general(p2, sel, dn, preferred_element_type=F32)
            + lax.dot_general(p3, sel, dn, preferred_element_type=F32))


def _sigmoid(x):
    return 1.0 / (1.0 + jnp.exp(-x))


def _silu(x):
    return x * _sigmoid(x)


def _softplus(x):
    return jnp.maximum(x, 0.0) + jnp.log(1.0 + jnp.exp(-jnp.abs(x)))


def _log_sigmoid(x):
    return -_softplus(-x)


def _tri_incl(n):
    r = lax.broadcasted_iota(jnp.int32, (n, n), 0)
    c = lax.broadcasted_iota(jnp.int32, (n, n), 1)
    return (c <= r).astype(F32)


def _rms_kernel(x_ref, g_ref, o_ref):
    x = x_ref[...]
    ms = jnp.mean(x * x, axis=-1, keepdims=True)
    o_ref[...] = (x * lax.rsqrt(ms + EPS) * g_ref[...]).astype(o_ref.dtype)


def rms_norm(x, g, out_dtype):
    t, d = x.shape
    tm = _row_tile(t, 520)
    return pl.pallas_call(
        _rms_kernel,
        grid=(t // tm,),
        in_specs=[pl.BlockSpec((tm, d), lambda i: (i, 0)),
                  pl.BlockSpec((1, d), lambda i: (0, 0))],
        out_specs=pl.BlockSpec((tm, d), lambda i: (i, 0)),
        out_shape=jax.ShapeDtypeStruct((t, d), out_dtype),
        compiler_params=_cparams(("parallel",)),
        name="rms_norm",
    )(x, g.reshape(1, d))


def _rms_split_kernel(x_ref, g_ref, head_ref, tail_ref, *, n_head):
    x = x_ref[...]
    ms = jnp.mean(x * x, axis=-1, keepdims=True)
    y = x * lax.rsqrt(ms + EPS) * g_ref[...]
    i = pl.program_id(0)

    @pl.when(i < n_head)
    def _():
        head_ref[...] = y

    @pl.when(i >= n_head)
    def _():
        tail_ref[...] = y


def rms_norm_split(x, g, n_first, tm):
    t, d = x.shape
    nh = n_first // tm
    return pl.pallas_call(
        functools.partial(_rms_split_kernel, n_head=nh),
        grid=(t // tm,),
        in_specs=[pl.BlockSpec((tm, d), lambda i: (i, 0)),
                  pl.BlockSpec((1, d), lambda i: (0, 0))],
        out_specs=[pl.BlockSpec((tm, d), lambda i: (jnp.minimum(i, nh - 1), 0)),
                   pl.BlockSpec((tm, d), lambda i: (jnp.maximum(i - nh, 0), 0))],
        out_shape=[jax.ShapeDtypeStruct((n_first, d), F32), jax.ShapeDtypeStruct((t - n_first, d), F32)],
        compiler_params=_cparams(("arbitrary",)),
        name="rms_norm_split",
    )(x, g.reshape(1, d))


def _proj_kernel(x_ref, wt_ref, o_ref):
    o_ref[...] = lax.dot_general(x_ref[...], wt_ref[...], (((1,), (1,)), ((), ())),
                                 preferred_element_type=F32).astype(o_ref.dtype)


def project(x, wt, layer, out_dtype=F32):
    t, k = x.shape
    n = wt.shape[1]
    tm, tn = _row_tile(t, 1040), _col_tile(n)
    return pl.pallas_call(
        _proj_kernel,
        grid=(t // tm, n // tn),
        in_specs=[pl.BlockSpec((tm, k), lambda i, j: (i, 0)),
                  pl.BlockSpec((None, tn, k), lambda i, j: (layer, j, 0))],
        out_specs=pl.BlockSpec((tm, tn), lambda i, j: (i, j)),
        out_shape=jax.ShapeDtypeStruct((t, n), out_dtype),
        compiler_params=_cparams(("parallel", "arbitrary")),
        name="project",
    )(x, wt)


def _merge_kernel(o1, o2, o3, w1, w2, w3, g1, g2, g3, out):
    acc = _sigmoid(g1[...]) * jnp.dot(o1[...], w1[...], preferred_element_type=F32)
    acc += _sigmoid(g2[...]) * jnp.dot(o2[...], w2[...], preferred_element_type=F32)
    acc += _sigmoid(g3[...]) * jnp.dot(o3[...], w3[...], preferred_element_type=F32)
    out[...] = acc.astype(out.dtype)


def merge_branches(ys, w_branch, layer, u_gate):
    t, kb = ys[0].shape
    d = w_branch.shape[3]
    tm, tn = _row_tile(t, 1040), _col_tile(d, 512)
    nb = d // tn
    y_spec = pl.BlockSpec((tm, kb), lambda i, j: (i, 0), pipeline_mode=pl.Buffered(1))
    w_specs = [pl.BlockSpec((None, None, kb, tn), functools.partial(lambda i, j, b: (layer, b, 0, j), b=b))
               for b in range(3)]
    g_specs = [pl.BlockSpec((tm, tn), functools.partial(lambda i, j, b: (i, b * nb + j), b=b))
               for b in range(3)]
    return pl.pallas_call(
        _merge_kernel,
        grid=(t // tm, nb),
        in_specs=[y_spec, y_spec, y_spec] + w_specs + g_specs,
        out_specs=pl.BlockSpec((tm, tn), lambda i, j: (i, j)),
        out_shape=jax.ShapeDtypeStruct((t, d), BF16),
        compiler_params=_cparams(("parallel", "arbitrary")),
        name="merge_branches",
    )(ys[0], ys[1], ys[2], w_branch, w_branch, w_branch, u_gate, u_gate, u_gate)


def _resid_kernel(a_ref, w_ref, x_ref, o_ref):
    o_ref[...] = x_ref[...] + jnp.dot(a_ref[...], w_ref[...].astype(BF16), preferred_element_type=F32)


def residual_matmul(a, w, layer, x):
    t, k = a.shape
    n = w.shape[2]
    tm, tn = _row_tile(t, 1040), _col_tile(n, 512)
    return pl.pallas_call(
        _resid_kernel,
        grid=(n // tn, t // tm),
        in_specs=[pl.BlockSpec((tm, k), lambda j, i: (i, 0)),
                  pl.BlockSpec((None, k, tn), lambda j, i: (layer, 0, j)),
                  pl.BlockSpec((tm, tn), lambda j, i: (i, j))],
        out_specs=pl.BlockSpec((tm, tn), lambda j, i: (i, j)),
        out_shape=jax.ShapeDtypeStruct((t, n), F32),
        compiler_params=_cparams(("parallel", "arbitrary")),
        name="residual_matmul",
    )(a, w, x)


def _ff1_kernel(a_ref, w_ref, o_ref):
    h = jnp.maximum(jnp.dot(a_ref[...], w_ref[...].astype(BF16), preferred_element_type=F32), 0.0)
    o_ref[...] = (h * h).astype(o_ref.dtype)


def ffn_up(a, w, layer):
    t, k = a.shape
    n = w.shape[2]
    tm, tn = _row_tile(t, 1040), _col_tile(n, 512)
    return pl.pallas_call(
        _ff1_kernel,
        grid=(n // tn, t // tm),
        in_specs=[pl.BlockSpec((tm, k), lambda j, i: (i, 0)),
                  pl.BlockSpec((None, k, tn), lambda j, i: (layer, 0, j))],
        out_specs=pl.BlockSpec((tm, tn), lambda j, i: (i, j)),
        out_shape=jax.ShapeDtypeStruct((t, n), BF16),
        compiler_params=_cparams(("parallel", "arbitrary")),
        name="ffn_up",
    )(a, w)


def _ff2_kernel(a_ref, w_ref, x_ref, o_ref, acc_ref):
    kk = pl.program_id(2)

    @pl.when(kk == 0)
    def _():
        acc_ref[...] = x_ref[...]

    acc_ref[...] += jnp.dot(a_ref[...], w_ref[...], preferred_element_type=F32)

    @pl.when(kk == pl.num_programs(2) - 1)
    def _():
        o_ref[...] = acc_ref[...]


def ffn_down(a, w, layer, x):
    t, k = a.shape
    n = w.shape[2]
    tm, tn, tk = _row_tile(t, 1040), _col_tile(n, 1024), _col_tile(k, 2048)
    return pl.pallas_call(
        _ff2_kernel,
        grid=(t // tm, n // tn, k // tk),
        in_specs=[pl.BlockSpec((tm, tk), lambda i, j, l: (i, l)),
                  pl.BlockSpec((None, tk, tn), lambda i, j, l: (layer, l, j)),
                  pl.BlockSpec((tm, tn), lambda i, j, l: (i, j))],
        out_specs=pl.BlockSpec((tm, tn), lambda i, j, l: (i, j)),
        out_shape=jax.ShapeDtypeStruct((t, n), F32),
        scratch_shapes=[pltpu.VMEM((tm, tn), F32)],
        compiler_params=_cparams(("parallel", "arbitrary", "arbitrary")),
        name="ffn_down",
    )(a, w, x)


def _ple_kernel(p_ref, wp_ref, h_ref, wg_ref, x_ref, o_ref):
    proj = jnp.dot(p_ref[...], wp_ref[...].astype(BF16), preferred_element_type=F32)
    gate = _sigmoid(jnp.dot(h_ref[...], wg_ref[...].astype(BF16), preferred_element_type=F32))
    o_ref[...] = x_ref[...] + proj * gate


def ple_update(p, wp, hn, wg, layer, x):
    t, kp = p.shape[1:]
    k = hn.shape[1]
    n = wg.shape[2]
    tm, tn = _row_tile(t, 1040), _col_tile(n, 512)
    return pl.pallas_call(
        _ple_kernel,
        grid=(n // tn, t // tm),
        in_specs=[pl.BlockSpec((None, tm, kp), lambda j, i: (layer, i, 0)),
                  pl.BlockSpec((None, kp, tn), lambda j, i: (layer, 0, j)),
                  pl.BlockSpec((tm, k), lambda j, i: (i, 0)),
                  pl.BlockSpec((None, k, tn), lambda j, i: (layer, 0, j)),
                  pl.BlockSpec((tm, tn), lambda j, i: (i, j))],
        out_specs=pl.BlockSpec((tm, tn), lambda j, i: (i, j)),
        out_shape=jax.ShapeDtypeStruct((t, n), F32),
        compiler_params=_cparams(("parallel", "arbitrary")),
        name="ple_update",
    )(p, wp, hn, wg, x)


def _gla_gates(fl, fup_ref, fb_ref):
    return _log_sigmoid(_bdot(fl, fup_ref[...]) + fb_ref[...]) * (1.0 / GLA_TAU)


def _gla_finish(o, g, nw_ref):
    ms = jnp.mean(o * o, axis=-1, keepdims=True)
    return o * lax.rsqrt(ms + EPS) * nw_ref[...] * _silu(g)


def _prompt_rows(n_seq, seq_len, n_rows):
    nc = seq_len // CH
    tail = n_rows - n_seq * seq_len
    assert tail % CH == 0 and tail // CH <= n_seq
    nt = tail // CH
    steps = nc + (1 if nt else 0)
    u_map = lambda b, c: (b * nc + jnp.minimum(c, nc - 1), 0)
    y_map = lambda b, c: (jnp.where(c < nc, b * nc + c, jnp.where(b < nt, n_seq * nc + b, b * nc + nc - 1)), 0)
    return (nc, nt), steps, u_map, y_map


def _chunk_kernel(body, nc_nt, y_index):
    nc, nt = nc_nt

    def kern(*refs):
        b = pl.program_id(0)
        c = pl.program_id(1)

        @pl.when(c < nc)
        def _():
            body(c, *refs)

        @pl.when(jnp.logical_and(c >= nc, b < nt))
        def _():
            refs[y_index][...] = jnp.zeros_like(refs[y_index])

    return kern


def _insert_kernel(rows_ref, buf_ref, o_ref):
    o_ref[...] = rows_ref[...].astype(o_ref.dtype)


def insert_rows(buf, rows, row0):
    n, w = rows.shape
    assert row0 % n == 0
    return pl.pallas_call(
        _insert_kernel,
        grid=(1,),
        in_specs=[pl.BlockSpec((n, w), lambda i: (0, 0)), pl.BlockSpec(memory_space=pl.ANY)],
        out_specs=pl.BlockSpec((n, w), lambda i: (row0 // n, 0)),
        out_shape=jax.ShapeDtypeStruct(buf.shape, buf.dtype),
        input_output_aliases={1: 0},
        compiler_params=_cparams(("arbitrary",)),
        name="insert_rows",
    )(rows, buf)


def _gla_prompt_kernel(c, u_ref, fup_ref, fb_ref, nw_ref, y_ref, s_ref):
    @pl.when(c == 0)
    def _():
        s_ref[...] = jnp.zeros_like(s_ref)

    lg = _gla_gates(u_ref[:, GL_F:GL_F + 128], fup_ref, fb_ref)
    tri = _tri_incl(CH)
    bcum = _xdot_l(tri, lg)
    ones_c = jnp.ones((CH, 128), F32)
    blast = bcum[CH - 1:CH, :]
    e_pos = jnp.exp(bcum)
    e_neg = jnp.exp(-bcum)
    e_end = jnp.exp(blast - bcum)
    for h in range(GLA_H):
        ks = slice(h * GLA_DK, (h + 1) * GLA_DK)
        vs = slice(h * GLA_DV, (h + 1) * GLA_DV)
        q = u_ref[:, GL_Q + h * GLA_DK:GL_Q + (h + 1) * GLA_DK] * (GLA_DK ** -0.5)
        k = u_ref[:, GL_K + h * GLA_DK:GL_K + (h + 1) * GLA_DK]
        v = u_ref[:, GL_V + h * GLA_DV:GL_V + (h + 1) * GLA_DV]
        g = u_ref[:, GL_G + h * GLA_DV:GL_G + (h + 1) * GLA_DV]
        q_in = q * e_pos[:, ks]
        k_in = k * e_neg[:, ks]
        k_end = k * e_end[:, ks]
        att = _bdot_nt(q_in, k_in) * tri
        s_prev = s_ref[h]
        o = _bdot(att, v) + _bdot(q_in, s_prev)
        dec = jnp.exp(_xdot_tn(lg[:, ks], ones_c))
        s_ref[h] = s_prev * jnp.concatenate([dec] * (GLA_DV // 128), axis=1) + _bdot_tn(k_end, v)
        y_ref[:, vs] = _gla_finish(o, g, nw_ref).astype(y_ref.dtype)


def gla_prompt(u, n_seq, seq_len, f_up, f_bias, norm_w):
    nc, steps, u_map, y_map = _prompt_rows(n_seq, seq_len, u.shape[0])
    y, s = pl.pallas_call(
        _chunk_kernel(_gla_prompt_kernel, nc, 4),
        grid=(n_seq, steps),
        in_specs=[pl.BlockSpec((CH, GL_PAD), u_map),
                  pl.BlockSpec((128, GLA_K), lambda b, c: (0, 0)),
                  pl.BlockSpec((1, GLA_K), lambda b, c: (0, 0)),
                  pl.BlockSpec((1, GLA_DV), lambda b, c: (0, 0))],
        out_specs=[pl.BlockSpec((CH, BR), y_map),
                   pl.BlockSpec((None, GLA_H, GLA_DK, GLA_DV), lambda b, c: (b, 0, 0, 0))],
        out_shape=[jax.ShapeDtypeStruct((u.shape[0], BR), BF16),
                   jax.ShapeDtypeStruct((n_seq, GLA_H, GLA_DK, GLA_DV), F32)],
        compiler_params=_cparams(("parallel", "arbitrary")),
        name="gla_prompt",
    )(u, f_up, f_bias, norm_w)
    return y, s


def _col_of_row(row, eye):
    return jnp.sum(eye * row, axis=-1, keepdims=True)


def _gla_sample_kernel(q_ref, k_ref, v_ref, g_ref, fl_ref, s_ref, fup_ref, fb_ref, nw_ref, y_ref, so_ref):
    nb = q_ref.shape[0]
    alpha = jnp.exp(_gla_gates(fl_ref[...], fup_ref, fb_ref))
    r = lax.broadcasted_iota(jnp.int32, (GLA_DK, GLA_DK), 0)
    c = lax.broadcasted_iota(jnp.int32, (GLA_DK, GLA_DK), 1)
    eye = (r == c).astype(F32)
    for b in range(nb):
        q_col = _col_of_row(q_ref[b:b + 1, :] * (GLA_DK ** -0.5), eye)
        k_col = _col_of_row(k_ref[b:b + 1, :], eye)
        a_col = _col_of_row(alpha[b:b + 1, :], eye)
        s_new = s_ref[b] * a_col + k_col * v_ref[b:b + 1, :]
        so_ref[b] = s_new
        o = jnp.sum(q_col * s_new, axis=0, keepdims=True)
        y_ref[b:b + 1, :] = _gla_finish(o, g_ref[b:b + 1, :], nw_ref)


def _state_chain(kernel_fn, n_in, depth, prev):
    if prev is None:
        def first(*refs):
            state_out = refs[n_in + 1]
            kernel_fn(*refs[:n_in + 1], state_out.at[0], *refs[n_in + 2:])
            for later in range(1, depth):
                state_out[later] = jnp.zeros(state_out.shape[1:], state_out.dtype)

        return first, [], [], {}, depth

    def chained(*refs):
        kernel_fn(*refs[:n_in], *refs[n_in + 1:])

    return chained, [pl.BlockSpec(memory_space=pl.ANY)], [prev], {n_in: 1}, None


def gla_sample(u, row0, n_tok, state, layer, prev, f_up, f_bias, norm_w):
    nb = 8
    r0 = row0 // nb
    col = lambda base, w: (lambda i, h: (r0 + i, base // w + h))
    kern, xspec, xarg, alias, lead = _state_chain(_gla_sample_kernel, 9, state.shape[0], prev)
    y, s = pl.pallas_call(
        kern,
        grid=(n_tok // nb, GLA_H),
        input_output_aliases=alias,
        in_specs=[pl.BlockSpec((nb, GLA_DK), col(GL_Q, GLA_DK)),
                  pl.BlockSpec((nb, GLA_DK), col(GL_K, GLA_DK)),
                  pl.BlockSpec((nb, GLA_DV), col(GL_V, GLA_DV)),
                  pl.BlockSpec((nb, GLA_DV), col(GL_G, GLA_DV)),
                  pl.BlockSpec((nb, 128), lambda i, h: (r0 + i, GL_F // 128)),
                  pl.BlockSpec((None, nb, None, GLA_DK, GLA_DV), lambda i, h: (layer, i, h, 0, 0)),
                  pl.BlockSpec((128, GLA_DK), lambda i, h: (0, h)),
                  pl.BlockSpec((1, GLA_DK), lambda i, h: (0, h)),
                  pl.BlockSpec((1, GLA_DV), lambda i, h: (0, 0))] + xspec,
        out_specs=[pl.BlockSpec((nb, GLA_DV), lambda i, h: (i, h)),
                   pl.BlockSpec((lead, nb, None, GLA_DK, GLA_DV),
                                lambda i, h: (0 if lead else layer, i, h, 0, 0))],
        out_shape=[jax.ShapeDtypeStruct((n_tok, BR), F32),
                   jax.ShapeDtypeStruct(state.shape, F32)],
        compiler_params=_cparams(("parallel", "parallel")),
        name="gla_sample",
    )(u, u, u, u, u, state, f_up, f_bias, norm_w, *xarg)
    return y, s


def _head_expand_matrix():
    r = lax.broadcasted_iota(jnp.int32, (128, BR), 0)
    c = lax.broadcasted_iota(jnp.int32, (128, BR), 1)
    return (r == c // HD).astype(F32)


def _stack4(x, mask4):
    return jnp.concatenate([x, x, x, x], axis=0) * mask4


def _quad_masks():
    r = lax.broadcasted_iota(jnp.int32, (4 * CH, QW), 0)
    c = lax.broadcasted_iota(jnp.int32, (4 * CH, QW), 1)
    return (r // CH == c // HD).astype(F32)


def _ssm_prompt_kernel(c, u_ref, cw_ref, cb_ref, dtb_ref, a_ref, d_ref, nw_ref, y_ref, h_ref,
                       carry_ref, x_s, dtl_s, cum_s, lal_s, yo_s):
    @pl.when(c == 0)
    def _():
        h_ref[...] = jnp.zeros_like(h_ref)
        carry_ref[...] = jnp.zeros_like(carry_ref)

    pre = u_ref[:, SS_X:SS_X + SSM_XBC]
    prev = carry_ref[...]
    row8 = lax.broadcasted_iota(jnp.int32, (8, SSM_XBC), 0)
    acc = cb_ref[...] + pre * cw_ref[SSM_CONV - 1:SSM_CONV, :]
    for j in range(1, SSM_CONV):
        sh = pltpu.roll(pre, j, axis=0)
        top = jnp.where(row8 < j, pltpu.roll(prev, j, axis=0), sh[0:8, :])
        sh = jnp.concatenate([top, sh[8:, :]], axis=0)
        acc = acc + sh * cw_ref[SSM_CONV - 1 - j:SSM_CONV - j, :]
    carry_ref[...] = pre[CH - 8:CH, :]
    xbc = _silu(acc)
    x_s[...] = xbc[:, 0:BR]

    expand = _head_expand_matrix()
    tri = _tri_incl(CH)
    dt = _softplus(u_ref[:, SS_DT:SS_DT + 128] + dtb_ref[...])
    la = dt * a_ref[...]
    dtl_s[...] = _xdot(dt, expand)
    lal_s[...] = _xdot(la, expand)
    cum_s[...] = _xdot(_xdot_l(tri, la), expand)

    mask4 = _quad_masks()
    t_i = lax.broadcasted_iota(jnp.int32, (CH, 4 * CH), 0)
    s_i = lax.broadcasted_iota(jnp.int32, (CH, 4 * CH), 1) % CH
    causal = s_i <= t_i
    upto = (t_i <= s_i).astype(F32)
    ones_cn = jnp.ones((CH, SSM_N), F32)
    for q in range(NQ):
        g = q // (NQ // SSM_G)
        ls = slice(q * QW, (q + 1) * QW)
        bm = xbc[:, BR + g * SSM_N:BR + (g + 1) * SSM_N]
        cm = xbc[:, BR + SSM_G * SSM_N + g * SSM_N:BR + SSM_G * SSM_N + (g + 1) * SSM_N]
        cum_q = cum_s[:, ls]
        lal_q = lal_s[:, ls]
        xdt = x_s[:, ls] * dtl_s[:, ls]
        cum_row = jnp.sum(lal_q * upto, axis=0, keepdims=True)
        lmat = jnp.exp(jnp.where(causal, cum_q - cum_row, -1e30))
        cb = _bdot_nt(cm, jnp.concatenate([bm, bm, bm, bm], axis=0))
        y_diag = _bdot(cb * lmat, _stack4(xdt, mask4))
        total = cum_q[CH - 1:CH, :]
        h_prev = h_ref[q * QW:(q + 1) * QW, :]
        y_off = _bdot_nt(cm, h_prev) * jnp.exp(cum_q)
        dec = jnp.exp(_xdot_tn(lal_q, ones_cn))
        h_ref[q * QW:(q + 1) * QW, :] = h_prev * dec + _bdot_tn(xdt * jnp.exp(total - cum_q), bm)
        yo_s[:, ls] = y_diag + y_off + d_ref[:, ls] * x_s[:, ls]

    gw = BR // SSM_G
    for g in range(SSM_G):
        gs = slice(g * gw, (g + 1) * gw)
        yg = yo_s[:, gs] * _silu(u_ref[:, SS_Z + g * gw:SS_Z + (g + 1) * gw])
        ms = jnp.mean(yg * yg, axis=-1, keepdims=True)
        y_ref[:, gs] = (yg * lax.rsqrt(ms + EPS) * nw_ref[:, gs]).astype(y_ref.dtype)


def ssm_prompt(u, n_seq, seq_len, conv_w, conv_b, dt_bias, a_neg, d_lane, norm_w):
    nc, steps, u_map, y_map = _prompt_rows(n_seq, seq_len, u.shape[0])
    const = lambda shape: pl.BlockSpec(shape, lambda b, c: (0, 0))
    y, h = pl.pallas_call(
        _chunk_kernel(_ssm_prompt_kernel, nc, 7),
        grid=(n_seq, steps),
        in_specs=[pl.BlockSpec((CH, SS_PAD), u_map),
                  const((SSM_CONV, SSM_XBC)), const((1, SSM_XBC)), const((1, 128)), const((1, 128)),
                  const((1, BR)), const((1, BR))],
        out_specs=[pl.BlockSpec((CH, BR), y_map),
                   pl.BlockSpec((None, BR, SSM_N), lambda b, c: (b, 0, 0))],
        out_shape=[jax.ShapeDtypeStruct((u.shape[0], BR), BF16),
                   jax.ShapeDtypeStruct((n_seq, BR, SSM_N), F32)],
        scratch_shapes=[pltpu.VMEM((8, SSM_XBC), F32)] + [pltpu.VMEM((CH, BR), F32)] * 5,
        compiler_params=_cparams(("parallel", "arbitrary")),
        name="ssm_prompt",
    )(u, conv_w, conv_b, dt_bias, a_neg, d_lane, norm_w)
    return y, h


def _ssm_sample_kernel(x_ref, b_ref, c_ref, dt_ref, z_ref, sx_ref, sb_ref, sc_ref,
                       cwx_ref, cwb_ref, cwc_ref, cbx_ref, cbb_ref, cbc_ref,
                       dtb_ref, a_ref, d_ref, nw_ref, h_ref, y_ref, ho_ref):
    nb = x_ref.shape[0]
    gw = x_ref.shape[1]

    def conv(cur_ref, st_ref, w_ref, bias_ref):
        acc = bias_ref[...] + cur_ref[...] * w_ref[SSM_CONV - 1:SSM_CONV, :]
        for j in range(SSM_CONV - 1):
            acc = acc + st_ref[j] * w_ref[j:j + 1, :]
        return _silu(acc)

    xs = conv(x_ref, sx_ref, cwx_ref, cbx_ref)
    bm = conv(b_ref, sb_ref, cwb_ref, cbb_ref)
    cm = conv(c_ref, sc_ref, cwc_ref, cbc_ref)
    g = pl.program_id(1)
    r = lax.broadcasted_iota(jnp.int32, (128, gw), 0)
    col = lax.broadcasted_iota(jnp.int32, (128, gw), 1)
    expand = (r == col // HD + g * (gw // HD)).astype(F32)
    dt = _softplus(dt_ref[...] + dtb_ref[...])
    dtl = _hdot(dt, expand)
    decl = jnp.exp(_hdot(dt * a_ref[...], expand))
    xdt = xs * dtl
    rr = lax.broadcasted_iota(jnp.int32, (gw, gw), 0)
    cc = lax.broadcasted_iota(jnp.int32, (gw, gw), 1)
    eye = (rr == cc).astype(F32)
    for b in range(nb):
        dec_col = _col_of_row(decl[b:b + 1, :], eye)
        xdt_col = _col_of_row(xdt[b:b + 1, :], eye)
        h_new = h_ref[b] * dec_col + xdt_col * bm[b:b + 1, :]
        ho_ref[b] = h_new
        y_col = jnp.sum(h_new * cm[b:b + 1, :], axis=-1, keepdims=True)
        y_row = jnp.sum(eye * y_col, axis=0, keepdims=True)
        y_row = (y_row + d_ref[...] * xs[b:b + 1, :]) * _silu(z_ref[b:b + 1, :])
        ms = jnp.mean(y_row * y_row, axis=-1, keepdims=True)
        y_ref[b:b + 1, :] = y_row * lax.rsqrt(ms + EPS) * nw_ref[...]


def ssm_sample(u, row0, n_tok, state, conv_state_t, layer, prev, conv_w, conv_b, dt_bias, a_neg, d_lane, norm_w):
    nb = 8
    kern, xspec, xarg, alias, lead = _state_chain(_ssm_sample_kernel, 19, state.shape[0], prev)
    r0 = row0 // nb
    gw = BR // SSM_G
    bb, cb = BR // SSM_N, (BR + SSM_G * SSM_N) // SSM_N
    ucol = lambda base, w: (lambda i, g: (r0 + i, base // w + g))
    scol = lambda base: (lambda i, g: (0, i, base + g))
    wcol = lambda base: (lambda i, g: (0, base + g))
    y, h = pl.pallas_call(
        kern,
        grid=(n_tok // nb, SSM_G),
        input_output_aliases=alias,
        in_specs=[pl.BlockSpec((nb, gw), ucol(SS_X, gw)),
                  pl.BlockSpec((nb, SSM_N), ucol(SS_B, SSM_N)),
                  pl.BlockSpec((nb, SSM_N), ucol(SS_C, SSM_N)),
                  pl.BlockSpec((nb, 128), lambda i, g: (r0 + i, SS_DT // 128)),
                  pl.BlockSpec((nb, gw), ucol(SS_Z, gw)),
                  pl.BlockSpec((SSM_CONV - 1, nb, gw), lambda i, g: (0, i, g)),
                  pl.BlockSpec((SSM_CONV - 1, nb, SSM_N), scol(bb)),
                  pl.BlockSpec((SSM_CONV - 1, nb, SSM_N), scol(cb)),
                  pl.BlockSpec((SSM_CONV, gw), lambda i, g: (0, g)),
                  pl.BlockSpec((SSM_CONV, SSM_N), wcol(bb)),
                  pl.BlockSpec((SSM_CONV, SSM_N), wcol(cb)),
                  pl.BlockSpec((1, gw), lambda i, g: (0, g)),
                  pl.BlockSpec((1, SSM_N), wcol(bb)),
                  pl.BlockSpec((1, SSM_N), wcol(cb)),
                  pl.BlockSpec((1, 128), lambda i, g: (0, 0)),
                  pl.BlockSpec((1, 128), lambda i, g: (0, 0)),
                  pl.BlockSpec((1, gw), lambda i, g: (0, g)),
                  pl.BlockSpec((1, gw), lambda i, g: (0, g)),
                  pl.BlockSpec((None, nb, None, gw, SSM_N), lambda i, g: (layer, i, g, 0, 0))] + xspec,
        out_specs=[pl.BlockSpec((nb, gw), lambda i, g: (i, g)),
                   pl.BlockSpec((lead, nb, None, gw, SSM_N), lambda i, g: (0 if lead else layer, i, g, 0, 0))],
        out_shape=[jax.ShapeDtypeStruct((n_tok, BR), F32),
                   jax.ShapeDtypeStruct(state.shape, F32)],
        compiler_params=_cparams(("parallel", "parallel")),
        name="ssm_sample",
    )(u, u, u, u, u, conv_state_t, conv_state_t, conv_state_t, conv_w, conv_w, conv_w,
      conv_b, conv_b, conv_b, dt_bias, a_neg, d_lane, norm_w, state, *xarg)
    return y, h


def _rw_mix_inputs(xs_of, w0_ref, w2_ref, a0_ref, a2_ref, g2_ref):
    r = xs_of(RW_R, BR)
    k = xs_of(RW_K, BR)
    v = xs_of(RW_V, BR)
    w = -_softplus(-(w0_ref[...] + _bdot(jnp.tanh(xs_of(RW_WL, 128)), w2_ref[...]))) - 0.5
    log_w = -jnp.exp(w)
    a = _sigmoid(a0_ref[...] + _bdot(xs_of(RW_AL, 128), a2_ref[...]))
    g = _bdot(_sigmoid(xs_of(RW_GL, RW_LG)), g2_ref[...])
    return r, k, v, log_w, a, g


def _seg_sum(x, ones_bd):
    hi = x.astype(BF16)
    lo = (x - hi.astype(F32)).astype(BF16)
    return (jnp.dot(hi, ones_bd, preferred_element_type=F32)
            + jnp.dot(lo, ones_bd, preferred_element_type=F32))


def _rw_prompt_kernel(c, u_ref, mu_ref, w0_ref, w2_ref, a0_ref, a2_ref, g2_ref, kk_ref, ka_ref, rk_ref,
                      lnw_ref, lnb_ref, y_ref, s_ref, carry_ref, r_s, k_s, v_s, lw_s, a_s, g_s):
    @pl.when(c == 0)
    def _():
        s_ref[...] = jnp.zeros_like(s_ref)
        carry_ref[...] = jnp.zeros_like(carry_ref)

    row8 = lax.broadcasted_iota(jnp.int32, (8, 1), 0)

    def xs_of(c0, width):
        cur = u_ref[:, c0:c0 + width]
        sh = pltpu.roll(cur, 1, axis=0)
        top = jnp.where(row8 < 1, pltpu.roll(carry_ref[:, c0:c0 + width], 1, axis=0), sh[0:8, :])
        prev = jnp.concatenate([top, sh[8:, :]], axis=0)
        return cur + (prev - cur) * mu_ref[:, c0:c0 + width]

    r, k, v, log_w, a, g = _rw_mix_inputs(xs_of, w0_ref, w2_ref, a0_ref, a2_ref, g2_ref)
    r_s[...] = r
    k_s[...] = k
    v_s[...] = v
    lw_s[...] = log_w
    a_s[...] = a
    g_s[...] = g
    carry_ref[...] = u_ref[CH - 8:CH, :]

    tri = _tri_incl(CH)
    mask4 = _quad_masks()
    rb = lax.broadcasted_iota(jnp.int32, (QW, QW), 0)
    cb = lax.broadcasted_iota(jnp.int32, (QW, QW), 1)
    bd = (rb // HD == cb // HD)
    bdf = bd.astype(F32)
    ones_bd = bd.astype(BF16)
    t_i = lax.broadcasted_iota(jnp.int32, (CH, 4 * CH), 0)
    s_i = lax.broadcasted_iota(jnp.int32, (CH, 4 * CH), 1) % CH
    strict = (s_i < t_i).astype(F32)
    incl = (s_i <= t_i).astype(F32)
    ident = (s_i == t_i).astype(F32)

    def blockdiag(x):
        return jnp.concatenate([x, x, x, x], axis=0) * bdf

    qs = range(NQ)
    ls = [slice(q * QW, (q + 1) * QW) for q in qs]
    stk = lambda x: _stack4(x, mask4)
    r_q = [r_s[:, ls[q]] for q in qs]
    v_q = [v_s[:, ls[q]] for q in qs]
    a_q = [a_s[:, ls[q]] for q in qs]
    kk = [k_s[:, ls[q]] * kk_ref[:, ls[q]] for q in qs]
    kn = [_seg_sum(kk[q] * kk[q], ones_bd) for q in qs]
    cum = [_xdot_l(tri, lw_s[:, ls[q]]) for q in qs]
    kk = [kk[q] / jnp.maximum(jnp.sqrt(kn[q]), 1e-12) for q in qs]
    kmod = [k_s[:, ls[q]] * (1.0 + (a_q[q] - 1.0) * ka_ref[:, ls[q]]) for q in qs]
    bv = [kk[q] * a_q[q] for q in qs]
    total = [cum[q][CH - 1:CH, :] for q in qs]
    e_neg = [jnp.exp(-cum[q]) for q in qs]
    e_end = [jnp.exp(total[q] - cum[q]) for q in qs]
    at = [-kk[q] * jnp.exp(cum[q] - lw_s[:, ls[q]]) for q in qs]
    rt = [r_q[q] * jnp.exp(cum[q]) for q in qs]
    bbar = [bv[q] * e_end[q] for q in qs]
    kbar = [kmod[q] * e_end[q] for q in qs]

    gram = [_bdot_nt(jnp.concatenate([at[q], rt[q]], axis=0),
                     jnp.concatenate([stk(bv[q] * e_neg[q]), stk(kmod[q] * e_neg[q])], axis=0))
            for q in qs]
    a_ab = [gram[q][0:CH, 0:4 * CH] * strict for q in qs]
    a_ak = [gram[q][0:CH, 4 * CH:8 * CH] * strict for q in qs]
    m_rbk = [jnp.concatenate([gram[q][CH:2 * CH, 0:4 * CH] * incl, gram[q][CH:2 * CH, 4 * CH:8 * CH] * incl],
                             axis=1) for q in qs]

    tinv = [ident + a_ab[q] for q in qs]
    p = a_ab
    for _ in range(5):
        p = [_bdot(p[q], blockdiag(p[q])) for q in qs]
        tinv = [tinv[q] + _bdot(tinv[q], blockdiag(p[q])) for q in qs]

    x1 = [_bdot(a_ak[q], stk(v_q[q])) for q in qs]
    a_hat = [_bdot(tinv[q], stk(at[q])) for q in qs]
    w_v = [_bdot(tinv[q], stk(x1[q])) for q in qs]
    q_hat = [rt[q] + _bdot(m_rbk[q][:, 0:4 * CH], stk(a_hat[q])) for q in qs]
    o_loc = [_bdot(m_rbk[q], jnp.concatenate([stk(w_v[q]), stk(v_q[q])], axis=0)) for q in qs]
    s_prev = [s_ref[q] for q in qs]
    o = [_bdot_nt(q_hat[q], s_prev[q]) + o_loc[q] for q in qs]
    g_corr = [_bdot_tn(a_hat[q], bbar[q]) * bdf for q in qs]
    h_new = [_bdot_tn(jnp.concatenate([w_v[q], v_q[q]], axis=0),
                      jnp.concatenate([bbar[q], kbar[q]], axis=0)) * bdf for q in qs]
    s_corr = [_bdot(s_prev[q], g_corr[q]) for q in qs]
    for q in qs:
        s_ref[q] = s_prev[q] * jnp.exp(total[q]) + s_corr[q] + h_new[q]

    mean = [_seg_sum(o[q], ones_bd) * (1.0 / HD) for q in qs]
    bsum = [_seg_sum(r_q[q] * kmod[q] * rk_ref[:, ls[q]], ones_bd) for q in qs]
    d = [o[q] - mean[q] for q in qs]
    var = [_seg_sum(d[q] * d[q], ones_bd) * (1.0 / HD) for q in qs]
    for q in qs:
        on = d[q] * lax.rsqrt(var[q] + RW_LN_EPS) * lnw_ref[:, ls[q]] + lnb_ref[:, ls[q]]
        y_ref[:, ls[q]] = ((on + bsum[q] * v_q[q]) * g_s[:, ls[q]]).astype(y_ref.dtype)


def rwkv_prompt(u, n_seq, seq_len, mu, w0, w2, a0, a2, g2, k_k, k_a, r_k, ln_w, ln_b):
    nc, steps, u_map, y_map = _prompt_rows(n_seq, seq_len, u.shape[0])
    const = lambda shape: pl.BlockSpec(shape, lambda b, c: (0, 0))
    y, s = pl.pallas_call(
        _chunk_kernel(_rw_prompt_kernel, nc, 12),
        grid=(n_seq, steps),
        in_specs=[pl.BlockSpec((CH, RW_PAD), u_map),
                  const((1, RW_PAD)), const((1, BR)), const((128, BR)), const((1, BR)), const((128, BR)),
                  const((RW_LG, BR)), const((1, BR)), const((1, BR)), const((1, BR)), const((1, BR)),
                  const((1, BR))],
        out_specs=[pl.BlockSpec((CH, BR), y_map),
                   pl.BlockSpec((None, NQ, QW, QW), lambda b, c: (b, 0, 0, 0))],
        out_shape=[jax.ShapeDtypeStruct((u.shape[0], BR), BF16),
                   jax.ShapeDtypeStruct((n_seq, NQ, QW, QW), F32)],
        scratch_shapes=[pltpu.VMEM((8, RW_PAD), F32)] + [pltpu.VMEM((CH, BR), F32)] * 6,
        compiler_params=_cparams(("parallel", "arbitrary")),
        name="rwkv_prompt",
    )(u, mu, w0, w2, a0, a2, g2, k_k, k_a, r_k, ln_w, ln_b)
    return y, s


def _rw_sample_pre_kernel(u_ref, sh_ref, mu_ref, w0_ref, w2_ref, a0_ref, a2_ref, g2_ref,
                          r_o, k_o, v_o, w_o, a_o, g_o):
    def xs_of(c0, width):
        cur = u_ref[:, c0:c0 + width]
        return cur + (sh_ref[:, c0:c0 + width] - cur) * mu_ref[:, c0:c0 + width]

    r, k, v, log_w, a, g = _rw_mix_inputs(xs_of, w0_ref, w2_ref, a0_ref, a2_ref, g2_ref)
    r_o[...] = r
    k_o[...] = k
    v_o[...] = v
    w_o[...] = jnp.exp(log_w)
    a_o[...] = a
    g_o[...] = g


def _rw_sample_step_kernel(r_ref, k_ref, v_ref, w_ref, a_ref, g_ref, kk_ref, ka_ref, rk_ref,
                           lnw_ref, lnb_ref, s_ref, y_ref, so_ref, o_scr):
    for h in range(r_ref.shape[0]):
        r, k, a = r_ref[h], k_ref[h], a_ref[h]
        w = w_ref[h]
        kk = k * kk_ref[h]
        kk = kk / jnp.maximum(jnp.sqrt(jnp.sum(kk * kk, axis=0, keepdims=True)), 1e-12)
        kmod = k * (1.0 + (a - 1.0) * ka_ref[h])
        av = -kk
        bv = kk * a

        def body(vi, carry):
            s = s_ref[h, vi]
            sa = jnp.sum(s * av, axis=0, keepdims=True)
            s_new = s * w + sa * bv + v_ref[h, pl.ds(vi, 1), :] * kmod
            so_ref[h, vi] = s_new
            o_scr[pl.ds(vi, 1), :] = jnp.sum(s_new * r, axis=0, keepdims=True)
            return carry

        lax.fori_loop(0, HD, body, 0)
        o = o_scr[...]
        mean = jnp.mean(o, axis=0, keepdims=True)
        d = o - mean
        var = jnp.mean(d * d, axis=0, keepdims=True)
        on = d * lax.rsqrt(var + RW_LN_EPS) * lnw_ref[h] + lnb_ref[h]
        bonus = jnp.sum(r * kmod * rk_ref[h], axis=0, keepdims=True) * v_ref[h]
        y_ref[h] = (on + bonus) * g_ref[h]


def rwkv_sample(u, row0, n_tok, state_t, shift_prev, layer, prev, mu, w0, w2, a0, a2, g2, k_k, k_a, r_k, ln_w, ln_b):
    const = lambda shape: pl.BlockSpec(shape, lambda i: (0, 0))
    vec = jax.ShapeDtypeStruct((n_tok, BR), F32)
    outs = pl.pallas_call(
        _rw_sample_pre_kernel,
        grid=(1,),
        in_specs=[pl.BlockSpec((n_tok, RW_PAD), lambda i: (row0 // n_tok, 0)),
                  pl.BlockSpec((None, n_tok, RW_PAD), lambda i: (layer, 0, 0)),
                  const((1, RW_PAD)), const((1, BR)), const((128, BR)), const((1, BR)), const((128, BR)),
                  const((RW_LG, BR))],
        out_specs=[pl.BlockSpec((n_tok, BR), lambda i: (0, 0))] * 6,
        out_shape=[vec] * 6,
        compiler_params=_cparams(("arbitrary",)),
        name="rwkv_sample_pre",
    )(u, shift_prev, mu, w0, w2, a0, a2, g2)
    outs = lax.optimization_barrier(outs)
    hb = 2
    vspec = pl.BlockSpec((hb, HD, n_tok), lambda i: (i, 0, 0))
    pspec = pl.BlockSpec((hb, HD, 1), lambda i: (i, 0, 0))
    kern, xspec, xarg, alias, lead = _state_chain(_rw_sample_step_kernel, 12, state_t.shape[0], prev)
    y, s = pl.pallas_call(
        kern,
        grid=(NH // hb,),
        input_output_aliases=alias,
        in_specs=[vspec] * 6 + [pspec] * 5
        + [pl.BlockSpec((None, hb, HD, HD, n_tok), lambda i: (layer, i, 0, 0, 0))] + xspec,
        out_specs=[vspec, pl.BlockSpec((lead, hb, HD, HD, n_tok), lambda i: (0 if lead else layer, i, 0, 0, 0))],
        out_shape=[jax.ShapeDtypeStruct((NH, HD, n_tok), F32),
                   jax.ShapeDtypeStruct(state_t.shape, F32)],
        scratch_shapes=[pltpu.VMEM((HD, n_tok), F32)],
        compiler_params=_cparams(("parallel",)),
        name="rwkv_sample_step",
    )(*[t.T.reshape(NH, HD, n_tok) for t in outs],
      *[t.reshape(NH, HD, 1) for t in (k_k, k_a, r_k, ln_w, ln_b)], state_t, *xarg)
    return y.reshape(BR, n_tok).T, s


def _pad_last(t, width):
    return jnp.pad(t, [(0, 0)] * (t.ndim - 1) + [(0, width - t.shape[-1])])


def _rw_to_padded(t):
    r_k_v = t[..., :3 * BR]
    wl = t[..., 3 * BR:3 * BR + RW_LW]
    al = t[..., 3 * BR + RW_LW:3 * BR + RW_LW + RW_LA]
    gl = t[..., 3 * BR + RW_LW + RW_LA:]
    return jnp.concatenate([r_k_v, _pad_last(wl, 128), _pad_last(al, 128), gl], axis=-1)


def _rw_from_padded(t):
    return jnp.concatenate([t[..., :RW_WL], t[..., RW_WL:RW_WL + RW_LW], t[..., RW_AL:RW_AL + RW_LA],
                            t[..., RW_GL:RW_GL + RW_LG]], axis=-1)


def _split_w_in(w_in):
    o1 = RW_COLS
    o2 = o1 + SS_COLS
    o3 = o2 + GL_COLS
    wt = jnp.transpose(w_in, (0, 2, 1)).astype(BF16)
    rows = lambda a, b: wt[:, a:b, :]
    zeros = lambda n: jnp.zeros((wt.shape[0], n, wt.shape[2]), BF16)
    cat = lambda parts: jnp.concatenate(parts, axis=1)
    w_rw = cat([rows(0, 3 * BR + RW_LW), zeros(128 - RW_LW),
                rows(3 * BR + RW_LW, 3 * BR + RW_LW + RW_LA), zeros(128 - RW_LA), rows(3 * BR + RW_LW + RW_LA, o1)])
    w_ss = cat([rows(o1, o2), zeros(SS_PAD - SS_COLS)])
    w_gl = cat([rows(o2, o3), zeros(GL_PAD - GL_COLS)])
    return w_rw, w_ss, w_gl, rows(o3, w_in.shape[2])


def _row(v):
    return v.reshape(1, -1).astype(F32)


def kernel(x_prompt, x_sample, p_prompt, p_sample, state_rwkv, state_shift, state_ssm, state_conv, state_gla, norm_mix, w_in, rw_mu, rw_w0, rw_w2, rw_a0, rw_a2, rw_g2, rw_kk, rw_ka, rw_rk, rw_ln_w, rw_ln_b, ssm_conv_w, ssm_conv_b, ssm_dt_bias, ssm_a_log, ssm_d, ssm_norm, gla_f_up, gla_f_bias, gla_norm, w_branch, w_out, norm_ffn, w_ff1, w_ff2, norm_ple, w_ple_gate, w_ple_proj, norm_final):
    depth = w_in.shape[0]
    bp, seq, d = x_prompt.shape
    ns = x_sample.shape[0]
    assert x_sample.shape[1] == 1, "the sample group is decoded one token at a time"
    assert seq % CH == 0 and seq >= 8
    tp = bp * seq
    assert tp % ns == 0 and ns % 8 == 0

    x = jnp.concatenate([x_prompt.reshape(tp, d), x_sample.reshape(ns, d)], axis=0)
    ple = jnp.concatenate([p_prompt.reshape(depth, tp, -1), p_sample.reshape(depth, ns, -1)], axis=1).astype(BF16)
    shift_pad = _rw_to_padded(state_shift)
    rw_state_t = jnp.transpose(state_rwkv, (0, 2, 3, 4, 1))
    ssm_state = state_ssm.reshape(depth, ns, SSM_G, BR // SSM_G, SSM_N)
    conv_t = jnp.transpose(state_conv, (0, 2, 1, 3))
    wb_branch, wb_ff2 = w_branch.astype(BF16), w_ff2.astype(BF16)

    outs = {k: [] for k in ("rw_p", "sh_p", "sh_s", "ss_p", "cv_p", "cv_s", "gl_p")}
    s_rw = s_ss = s_gl = None
    h = rms_norm(x, norm_mix[0], BF16)
    wb_rw, wb_ss, wb_gl, wb_gates = _split_w_in(w_in)
    for i in range(depth):
        u_rw = project(h, wb_rw, i)
        u_ss = project(h, wb_ss, i)
        u_gl = project(h, wb_gl, i)
        u_gate = project(h, wb_gates, i)

        rw_par = (_rw_to_padded(_row(rw_mu[i])), _row(rw_w0[i]),
                  jnp.pad(rw_w2[i], ((0, 128 - RW_LW), (0, 0))).astype(BF16), _row(rw_a0[i]),
                  jnp.pad(rw_a2[i], ((0, 128 - RW_LA), (0, 0))).astype(BF16), rw_g2[i].astype(BF16),
                  _row(rw_kk[i]), _row(rw_ka[i]), _row(rw_rk[i]), _row(rw_ln_w[i]), _row(rw_ln_b[i]))
        y_rw_p, s_bd = rwkv_prompt(u_rw, bp, seq, *rw_par)
        y_rw_s, s_rw = rwkv_sample(u_rw, tp, ns, rw_state_t, shift_pad, i, s_rw, *rw_par)
        sb = s_bd.reshape(bp, NQ, 4, HD, 4, HD)
        outs["rw_p"].append(jnp.stack([sb[:, :, j, :, j, :] for j in range(4)], axis=2).reshape(bp, NH, HD, HD))
        outs["sh_p"].append(_rw_from_padded(u_rw[seq - 1:tp:seq]))
        outs["sh_s"].append(_rw_from_padded(u_rw[tp:]))

        pad128 = lambda v: jnp.pad(v.astype(F32), (0, 128 - v.shape[0])).reshape(1, 128)
        ss_par = (ssm_conv_w[i], _row(ssm_conv_b[i]), pad128(ssm_dt_bias[i]),
                  pad128(-jnp.exp(ssm_a_log[i].astype(F32))), _row(jnp.repeat(ssm_d[i], HD)), _row(ssm_norm[i]))
        y_ss_p, h_p = ssm_prompt(u_ss, bp, seq, *ss_par)
        y_ss_s, s_ss = ssm_sample(u_ss, tp, ns, ssm_state, conv_t[i], i, s_ss, *ss_par)
        outs["ss_p"].append(h_p.reshape(bp, NH, HD, SSM_N))
        outs["cv_p"].append(jnp.stack([u_ss[(b + 1) * seq - (SSM_CONV - 1):(b + 1) * seq, SS_X:SS_X + SSM_XBC]
                                       for b in range(bp)]))
        outs["cv_s"].append(jnp.concatenate([state_conv[i][:, 1:], u_ss[tp:, None, SS_X:SS_X + SSM_XBC]], axis=1))

        gl_par = (jnp.pad(gla_f_up[i], ((0, 128 - GLA_LORA), (0, 0))).astype(BF16), _row(gla_f_bias[i]),
                  _row(gla_norm[i]))
        y_gl_p, g_p = gla_prompt(u_gl, bp, seq, *gl_par)
        y_gl_s, s_gl = gla_sample(u_gl, tp, ns, state_gla, i, s_gl, *gl_par)
        outs["gl_p"].append(g_p)

        ys = [insert_rows(yp, ysm, tp) for yp, ysm in ((y_rw_p, y_rw_s), (y_ss_p, y_ss_s), (y_gl_p, y_gl_s))]
        merged = merge_branches(ys, wb_branch, i, u_gate)
        x = residual_matmul(merged, w_out, i, x)
        hf = rms_norm(x, norm_ffn[i], BF16)
        x = ffn_down(ffn_up(hf, w_ff1, i), wb_ff2, i, x)
        hn = rms_norm(x, norm_ple[i], BF16)
        x = ple_update(ple, w_ple_proj, hn, w_ple_gate, i, x)
        if i + 1 < depth:
            h = rms_norm(x, norm_mix[i + 1], BF16)

    y_p, y_s = rms_norm_split(x, norm_final, tp, ns)
    st = lambda k: jnp.stack(outs[k])
    return (y_p.reshape(bp, seq, d), y_s.reshape(ns, 1, d), st("rw_p"),
            jnp.transpose(s_rw, (0, 4, 1, 2, 3)), st("sh_p"), st("sh_s"),
            st("ss_p"), s_ss.reshape(depth, ns, NH, HD, SSM_N), st("cv_p"), st("cv_s"), st("gl_p"), s_gl)
```

```python
import functools
import math

import jax
import jax.numpy as jnp
from jax import lax
from jax.experimental import pallas as pl
from jax.experimental.pallas import tpu as pltpu

F32 = jnp.float32
BF16 = jnp.bfloat16

BR = 2048
HD = 64
NH = BR // HD
QW = 256
NQ = BR // QW
CH = 64
RW_LW, RW_LA, RW_LG = 96, 96, 256
RW_LN_EPS = 64e-5
SSM_G, SSM_N, SSM_CONV = 4, 128, 4
SSM_XBC = BR + 2 * SSM_G * SSM_N
GLA_H, GLA_DK, GLA_DV, GLA_LORA, GLA_TAU = 4, 256, 512, 16, 16.0
GLA_K = GLA_H * GLA_DK
EPS = 1e-6

RW_R, RW_K, RW_V, RW_WL, RW_AL, RW_GL, RW_PAD = 0, 2048, 4096, 6144, 6272, 6400, 6656
RW_COLS = 3 * BR + RW_LW + RW_LA + RW_LG
SS_Z, SS_X, SS_B, SS_C, SS_DT, SS_PAD = 0, 2048, 4096, 4608, 5120, 5376
SS_COLS = BR + SSM_XBC + NH
GL_Q, GL_K, GL_V, GL_G, GL_F, GL_PAD = 0, 1024, 2048, 4096, 6144, 6400
GL_COLS = 2 * GLA_K + 2 * BR + GLA_LORA

V7X_VMEM_BYTES = 64 * 1024 * 1024
VMEM_REQUEST = 56 * 1024 * 1024
HI = lax.Precision.HIGHEST


def _cparams(sem):
    return pltpu.CompilerParams(dimension_semantics=sem, vmem_limit_bytes=VMEM_REQUEST)


def _row_tile(n, cap, align=16):
    best = None
    for t in range(align, min(n, cap) + 1, align):
        if n % t == 0:
            best = t
    return best if best is not None else n


def _col_tile(n, cap=1280):
    best = None
    for t in range(256, min(n, cap) + 1, 256):
        if n % t == 0:
            best = t
    if best is None:
        for t in range(128, min(n, cap) + 1, 128):
            if n % t == 0:
                best = t
    return best if best is not None else n


def _bdot(a, b):
    return jnp.dot(a.astype(BF16), b.astype(BF16), preferred_element_type=F32)


def _bdot_nt(a, b):
    return lax.dot_general(a.astype(BF16), b.astype(BF16), (((1,), (1,)), ((), ())),
                           preferred_element_type=F32)


def _bdot_tn(a, b):
    return lax.dot_general(a.astype(BF16), b.astype(BF16), (((0,), (0,)), ((), ())),
                           preferred_element_type=F32)


def _hdot(a, b):
    return jnp.dot(a, b, precision=HI, preferred_element_type=F32)


def _hdot_tn(a, b):
    return lax.dot_general(a, b, (((0,), (0,)), ((), ())), precision=HI,
                           preferred_element_type=F32)


def _split3(x):
    p1 = x.astype(BF16)
    r1 = x - p1.astype(F32)
    p2 = r1.astype(BF16)
    p3 = (r1 - p2.astype(F32)).astype(BF16)
    return p1, p2, p3


def _xdot(a, sel):
    sel = sel.astype(BF16)
    p1, p2, p3 = _split3(a)
    return (jnp.dot(p1, sel, preferred_element_type=F32) + jnp.dot(p2, sel, preferred_element_type=F32)
            + jnp.dot(p3, sel, preferred_element_type=F32))


def _xdot_l(sel, b):
    sel = sel.astype(BF16)
    p1, p2, p3 = _split3(b)
    return (jnp.dot(sel, p1, preferred_element_type=F32) + jnp.dot(sel, p2, preferred_element_type=F32)
            + jnp.dot(sel, p3, preferred_element_type=F32))


def _xdot_tn(a, sel):
    sel = sel.astype(BF16)
    dn = (((0,), (0,)), ((), ()))
    p1, p2, p3 = _split3(a)
    return (lax.dot_general(p1, sel, dn, preferred_element_type=F32)
            + lax.dot_general(p2, sel, dn, preferred_element_type=F32)
            + lax.dot_general(p3, sel, dn, preferred_element_type=F32))


def _sigmoid(x):
    return 1.0 / (1.0 + jnp.exp(-x))


def _silu(x):
    return x * _sigmoid(x)


def _softplus(x):
    return jnp.maximum(x, 0.0) + jnp.log(1.0 + jnp.exp(-jnp.abs(x)))


def _log_sigmoid(x):
    return -_softplus(-x)


def _tri_incl(n):
    r = lax.broadcasted_iota(jnp.int32, (n, n), 0)
    c = lax.broadcasted_iota(jnp.int32, (n, n), 1)
    return (c <= r).astype(F32)


def _rms_kernel(x_ref, g_ref, o_ref):
    x = x_ref[...]
    ms = jnp.mean(x * x, axis=-1, keepdims=True)
    o_ref[...] = (x * lax.rsqrt(ms + EPS) * g_ref[...]).astype(o_ref.dtype)


def rms_norm(x, g, out_dtype):
    t, d = x.shape
    tm = _row_tile(t, 520)
    return pl.pallas_call(
        _rms_kernel,
        grid=(t // tm,),
        in_specs=[pl.BlockSpec((tm, d), lambda i: (i, 0)),
                  pl.BlockSpec((1, d), lambda i: (0, 0))],
        out_specs=pl.BlockSpec((tm, d), lambda i: (i, 0)),
        out_shape=jax.ShapeDtypeStruct((t, d), out_dtype),
        compiler_params=_cparams(("parallel",)),
        name="rms_norm",
    )(x, g.reshape(1, d))


def _rms_split_kernel(x_ref, g_ref, head_ref, tail_ref, *, n_head):
    x = x_ref[...]
    ms = jnp.mean(x * x, axis=-1, keepdims=True)
    y = x * lax.rsqrt(ms + EPS) * g_ref[...]
    i = pl.program_id(0)

    @pl.when(i < n_head)
    def _():
        head_ref[...] = y

    @pl.when(i >= n_head)
    def _():
        tail_ref[...] = y


def rms_norm_split(x, g, n_first, tm):
    t, d = x.shape
    nh = n_first // tm
    return pl.pallas_call(
        functools.partial(_rms_split_kernel, n_head=nh),
        grid=(t // tm,),
        in_specs=[pl.BlockSpec((tm, d), lambda i: (i, 0)),
                  pl.BlockSpec((1, d), lambda i: (0, 0))],
        out_specs=[pl.BlockSpec((tm, d), lambda i: (jnp.minimum(i, nh - 1), 0)),
                   pl.BlockSpec((tm, d), lambda i: (jnp.maximum(i - nh, 0), 0))],
        out_shape=[jax.ShapeDtypeStruct((n_first, d), F32), jax.ShapeDtypeStruct((t - n_first, d), F32)],
        compiler_params=_cparams(("arbitrary",)),
        name="rms_norm_split",
    )(x, g.reshape(1, d))


def _proj_kernel(x_ref, wt_ref, o_ref):
    o_ref[...] = lax.dot_general(x_ref[...], wt_ref[...], (((1,), (1,)), ((), ())),
                                 preferred_element_type=F32).astype(o_ref.dtype)


def project(x, wt, layer, out_dtype=F32):
    t, k = x.shape
    n = wt.shape[1]
    tm, tn = _row_tile(t, 1040), _col_tile(n)
    return pl.pallas_call(
        _proj_kernel,
        grid=(t // tm, n // tn),
        in_specs=[pl.BlockSpec((tm, k), lambda i, j: (i, 0)),
                  pl.BlockSpec((None, tn, k), lambda i, j: (layer, j, 0))],
        out_specs=pl.BlockSpec((tm, tn), lambda i, j: (i, j)),
        out_shape=jax.ShapeDtypeStruct((t, n), out_dtype),
        compiler_params=_cparams(("parallel", "arbitrary")),
        name="project",
    )(x, wt)


def _merge_kernel(o1, o2, o3, w1, w2, w3, g1, g2, g3, out):
    acc = _sigmoid(g1[...]) * jnp.dot(o1[...], w1[...], preferred_element_type=F32)
    acc += _sigmoid(g2[...]) * jnp.dot(o2[...], w2[...], preferred_element_type=F32)
    acc += _sigmoid(g3[...]) * jnp.dot(o3[...], w3[...], preferred_element_type=F32)
    out[...] = acc.astype(out.dtype)


def merge_branches(ys, w_branch, layer, u_gate):
    t, kb = ys[0].shape
    d = w_branch.shape[3]
    tm, tn = _row_tile(t, 1040), _col_tile(d, 512)
    nb = d // tn
    y_spec = pl.BlockSpec((tm, kb), lambda i, j: (i, 0), pipeline_mode=pl.Buffered(1))
    w_specs = [pl.BlockSpec((None, None, kb, tn), functools.partial(lambda i, j, b: (layer, b, 0, j), b=b))
               for b in range(3)]
    g_specs = [pl.BlockSpec((tm, tn), functools.partial(lambda i, j, b: (i, b * nb + j), b=b))
               for b in range(3)]
    return pl.pallas_call(
        _merge_kernel,
        grid=(t // tm, nb),
        in_specs=[y_spec, y_spec, y_spec] + w_specs + g_specs,
        out_specs=pl.BlockSpec((tm, tn), lambda i, j: (i, j)),
        out_shape=jax.ShapeDtypeStruct((t, d), BF16),
        compiler_params=_cparams(("parallel", "arbitrary")),
        name="merge_branches",
    )(ys[0], ys[1], ys[2], w_branch, w_branch, w_branch, u_gate, u_gate, u_gate)


def _resid_kernel(a_ref, w_ref, x_ref, o_ref):
    o_ref[...] = x_ref[...] + jnp.dot(a_ref[...], w_ref[...].astype(BF16), preferred_element_type=F32)


def residual_matmul(a, w, layer, x):
    t, k = a.shape
    n = w.shape[2]
    tm, tn = _row_tile(t, 1040), _col_tile(n, 512)
    return pl.pallas_call(
        _resid_kernel,
        grid=(n // tn, t // tm),
        in_specs=[pl.BlockSpec((tm, k), lambda j, i: (i, 0)),
                  pl.BlockSpec((None, k, tn), lambda j, i: (layer, 0, j)),
                  pl.BlockSpec((tm, tn), lambda j, i: (i, j))],
        out_specs=pl.BlockSpec((tm, tn), lambda j, i: (i, j)),
        out_shape=jax.ShapeDtypeStruct((t, n), F32),
        compiler_params=_cparams(("parallel", "arbitrary")),
        name="residual_matmul",
    )(a, w, x)


def _ff1_kernel(a_ref, w_ref, o_ref):
    h = jnp.maximum(jnp.dot(a_ref[...], w_ref[...].astype(BF16), preferred_element_type=F32), 0.0)
    o_ref[...] = (h * h).astype(o_ref.dtype)


def ffn_up(a, w, layer):
    t, k = a.shape
    n = w.shape[2]
    tm, tn = _row_tile(t, 1040), _col_tile(n, 1024)
    return pl.pallas_call(
        _ff1_kernel,
        grid=(n // tn, t // tm),
        in_specs=[pl.BlockSpec((tm, k), lambda j, i: (i, 0)),
                  pl.BlockSpec((None, k, tn), lambda j, i: (layer, 0, j), pipeline_mode=pl.Buffered(1))],
        out_specs=pl.BlockSpec((tm, tn), lambda j, i: (i, j)),
        out_shape=jax.ShapeDtypeStruct((t, n), BF16),
        compiler_params=_cparams(("parallel", "arbitrary")),
        name="ffn_up",
    )(a, w)


def _ff2_kernel(a_ref, w_ref, x_ref, o_ref, acc_ref):
    kk = pl.program_id(2)

    @pl.when(kk == 0)
    def _():
        acc_ref[...] = x_ref[...]

    acc_ref[...] += jnp.dot(a_ref[...], w_ref[...], preferred_element_type=F32)

    @pl.when(kk == pl.num_programs(2) - 1)
    def _():
        o_ref[...] = acc_ref[...]


def ffn_down(a, w, layer, x):
    t, k = a.shape
    n = w.shape[2]
    tm, tn, tk = _row_tile(t, 1040), _col_tile(n, 1024), _col_tile(k, 2048)
    return pl.pallas_call(
        _ff2_kernel,
        grid=(t // tm, n // tn, k // tk),
        in_specs=[pl.BlockSpec((tm, tk), lambda i, j, l: (i, l)),
                  pl.BlockSpec((None, tk, tn), lambda i, j, l: (layer, l, j)),
                  pl.BlockSpec((tm, tn), lambda i, j, l: (i, j))],
        out_specs=pl.BlockSpec((tm, tn), lambda i, j, l: (i, j)),
        out_shape=jax.ShapeDtypeStruct((t, n), F32),
        scratch_shapes=[pltpu.VMEM((tm, tn), F32)],
        compiler_params=_cparams(("parallel", "arbitrary", "arbitrary")),
        name="ffn_down",
    )(a, w, x)


def _ple_kernel(p_ref, wp_ref, h_ref, wg_ref, x_ref, o_ref):
    proj = jnp.dot(p_ref[...], wp_ref[...].astype(BF16), preferred_element_type=F32)
    gate = _sigmoid(jnp.dot(h_ref[...], wg_ref[...].astype(BF16), preferred_element_type=F32))
    o_ref[...] = x_ref[...] + proj * gate


def ple_update(p, wp, hn, wg, layer, x):
    t, kp = p.shape[1:]
    k = hn.shape[1]
    n = wg.shape[2]
    tm, tn = _row_tile(t, 1040), _col_tile(n, 512)
    return pl.pallas_call(
        _ple_kernel,
        grid=(n // tn, t // tm),
        in_specs=[pl.BlockSpec((None, tm, kp), lambda j, i: (layer, i, 0)),
                  pl.BlockSpec((None, kp, tn), lambda j, i: (layer, 0, j)),
                  pl.BlockSpec((tm, k), lambda j, i: (i, 0)),
                  pl.BlockSpec((None, k, tn), lambda j, i: (layer, 0, j)),
                  pl.BlockSpec((tm, tn), lambda j, i: (i, j))],
        out_specs=pl.BlockSpec((tm, tn), lambda j, i: (i, j)),
        out_shape=jax.ShapeDtypeStruct((t, n), F32),
        compiler_params=_cparams(("parallel", "arbitrary")),
        name="ple_update",
    )(p, wp, hn, wg, x)


def _gla_gates(fl, fup_ref, fb_ref):
    return _log_sigmoid(_bdot(fl, fup_ref[...]) + fb_ref[...]) * (1.0 / GLA_TAU)


def _gla_finish(o, g, nw_ref):
    ms = jnp.mean(o * o, axis=-1, keepdims=True)
    return o * lax.rsqrt(ms + EPS) * nw_ref[...] * _silu(g)


def _prompt_rows(n_seq, seq_len, n_rows):
    nc = seq_len // CH
    tail = n_rows - n_seq * seq_len
    assert tail % CH == 0 and tail // CH <= n_seq
    nt = tail // CH
    steps = nc + (1 if nt else 0)
    u_map = lambda b, c: (b * nc + jnp.minimum(c, nc - 1), 0)
    y_map = lambda b, c: (jnp.where(c < nc, b * nc + c, jnp.where(b < nt, n_seq * nc + b, b * nc + nc - 1)), 0)
    return (nc, nt), steps, u_map, y_map


def _chunk_kernel(body, nc_nt, y_index):
    nc, nt = nc_nt

    def kern(*refs):
        b = pl.program_id(0)
        c = pl.program_id(1)

        @pl.when(c < nc)
        def _():
            body(c, *refs)

        @pl.when(jnp.logical_and(c >= nc, b < nt))
        def _():
            refs[y_index][...] = jnp.zeros_like(refs[y_index])

    return kern


def _insert_kernel(rows_ref, buf_ref, o_ref):
    o_ref[...] = rows_ref[...].astype(o_ref.dtype)


def insert_rows(buf, rows, row0):
    n, w = rows.shape
    assert row0 % n == 0
    return pl.pallas_call(
        _insert_kernel,
        grid=(1,),
        in_specs=[pl.BlockSpec((n, w), lambda i: (0, 0)), pl.BlockSpec(memory_space=pl.ANY)],
        out_specs=pl.BlockSpec((n, w), lambda i: (row0 // n, 0)),
        out_shape=jax.ShapeDtypeStruct(buf.shape, buf.dtype),
        input_output_aliases={1: 0},
        compiler_params=_cparams(("arbitrary",)),
        name="insert_rows",
    )(rows, buf)


def _gla_prompt_kernel(c, u_ref, fup_ref, fb_ref, nw_ref, y_ref, s_ref):
    @pl.when(c == 0)
    def _():
        s_ref[...] = jnp.zeros_like(s_ref)

    lg = _gla_gates(u_ref[:, GL_F:GL_F + 128], fup_ref, fb_ref)
    tri = _tri_incl(CH)
    bcum = _xdot_l(tri, lg)
    ones_c = jnp.ones((CH, 128), F32)
    blast = bcum[CH - 1:CH, :]
    e_pos = jnp.exp(bcum)
    e_neg = jnp.exp(-bcum)
    e_end = jnp.exp(blast - bcum)
    for h in range(GLA_H):
        ks = slice(h * GLA_DK, (h + 1) * GLA_DK)
        vs = slice(h * GLA_DV, (h + 1) * GLA_DV)
        q = u_ref[:, GL_Q + h * GLA_DK:GL_Q + (h + 1) * GLA_DK] * (GLA_DK ** -0.5)
        k = u_ref[:, GL_K + h * GLA_DK:GL_K + (h + 1) * GLA_DK]
        v = u_ref[:, GL_V + h * GLA_DV:GL_V + (h + 1) * GLA_DV]
        g = u_ref[:, GL_G + h * GLA_DV:GL_G + (h + 1) * GLA_DV]
        q_in = q * e_pos[:, ks]
        k_in = k * e_neg[:, ks]
        k_end = k * e_end[:, ks]
        att = _bdot_nt(q_in, k_in) * tri
        s_prev = s_ref[h]
        o = _bdot(att, v) + _bdot(q_in, s_prev)
        dec = jnp.exp(_xdot_tn(lg[:, ks], ones_c))
        s_ref[h] = s_prev * jnp.concatenate([dec] * (GLA_DV // 128), axis=1) + _bdot_tn(k_end, v)
        y_ref[:, vs] = _gla_finish(o, g, nw_ref).astype(y_ref.dtype)


def gla_prompt(u, n_seq, seq_len, f_up, f_bias, norm_w):
    nc, steps, u_map, y_map = _prompt_rows(n_seq, seq_len, u.shape[0])
    y, s = pl.pallas_call(
        _chunk_kernel(_gla_prompt_kernel, nc, 4),
        grid=(n_seq, steps),
        in_specs=[pl.BlockSpec((CH, GL_PAD), u_map),
                  pl.BlockSpec((128, GLA_K), lambda b, c: (0, 0)),
                  pl.BlockSpec((1, GLA_K), lambda b, c: (0, 0)),
                  pl.BlockSpec((1, GLA_DV), lambda b, c: (0, 0))],
        out_specs=[pl.BlockSpec((CH, BR), y_map),
                   pl.BlockSpec((None, GLA_H, GLA_DK, GLA_DV), lambda b, c: (b, 0, 0, 0))],
        out_shape=[jax.ShapeDtypeStruct((u.shape[0], BR), BF16),
                   jax.ShapeDtypeStruct((n_seq, GLA_H, GLA_DK, GLA_DV), F32)],
        compiler_params=_cparams(("parallel", "arbitrary")),
        name="gla_prompt",
    )(u, f_up, f_bias, norm_w)
    return y, s


def _col_of_row(row, eye):
    return jnp.sum(eye * row, axis=-1, keepdims=True)


def _gla_sample_kernel(q_ref, k_ref, v_ref, g_ref, fl_ref, s_ref, fup_ref, fb_ref, nw_ref, y_ref, so_ref):
    nb = q_ref.shape[0]
    alpha = jnp.exp(_gla_gates(fl_ref[...], fup_ref, fb_ref))
    r = lax.broadcasted_iota(jnp.int32, (GLA_DK, GLA_DK), 0)
    c = lax.broadcasted_iota(jnp.int32, (GLA_DK, GLA_DK), 1)
    eye = (r == c).astype(F32)
    for b in range(nb):
        q_col = _col_of_row(q_ref[b:b + 1, :] * (GLA_DK ** -0.5), eye)
        k_col = _col_of_row(k_ref[b:b + 1, :], eye)
        a_col = _col_of_row(alpha[b:b + 1, :], eye)
        s_new = s_ref[b] * a_col + k_col * v_ref[b:b + 1, :]
        so_ref[b] = s_new
        o = jnp.sum(q_col * s_new, axis=0, keepdims=True)
        y_ref[b:b + 1, :] = _gla_finish(o, g_ref[b:b + 1, :], nw_ref)


def _state_chain(kernel_fn, n_in, depth, prev):
    if prev is None:
        def first(*refs):
            state_out = refs[n_in + 1]
            kernel_fn(*refs[:n_in + 1], state_out.at[0], *refs[n_in + 2:])
            for later in range(1, depth):
                state_out[later] = jnp.zeros(state_out.shape[1:], state_out.dtype)

        return first, [], [], {}, depth

    def chained(*refs):
        kernel_fn(*refs[:n_in], *refs[n_in + 1:])

    return chained, [pl.BlockSpec(memory_space=pl.ANY)], [prev], {n_in: 1}, None


def gla_sample(u, row0, n_tok, state, layer, prev, f_up, f_bias, norm_w):
    nb = 8
    r0 = row0 // nb
    col = lambda base, w: (lambda i, h: (r0 + i, base // w + h))
    kern, xspec, xarg, alias, lead = _state_chain(_gla_sample_kernel, 9, state.shape[0], prev)
    y, s = pl.pallas_call(
        kern,
        grid=(n_tok // nb, GLA_H),
        input_output_aliases=alias,
        in_specs=[pl.BlockSpec((nb, GLA_DK), col(GL_Q, GLA_DK)),
                  pl.BlockSpec((nb, GLA_DK), col(GL_K, GLA_DK)),
                  pl.BlockSpec((nb, GLA_DV), col(GL_V, GLA_DV)),
                  pl.BlockSpec((nb, GLA_DV), col(GL_G, GLA_DV)),
                  pl.BlockSpec((nb, 128), lambda i, h: (r0 + i, GL_F // 128)),
                  pl.BlockSpec((None, nb, None, GLA_DK, GLA_DV), lambda i, h: (layer, i, h, 0, 0)),
                  pl.BlockSpec((128, GLA_DK), lambda i, h: (0, h)),
                  pl.BlockSpec((1, GLA_DK), lambda i, h: (0, h)),
                  pl.BlockSpec((1, GLA_DV), lambda i, h: (0, 0))] + xspec,
        out_specs=[pl.BlockSpec((nb, GLA_DV), lambda i, h: (i, h)),
                   pl.BlockSpec((lead, nb, None, GLA_DK, GLA_DV),
                                lambda i, h: (0 if lead else layer, i, h, 0, 0))],
        out_shape=[jax.ShapeDtypeStruct((n_tok, BR), F32),
                   jax.ShapeDtypeStruct(state.shape, F32)],
        compiler_params=_cparams(("parallel", "parallel")),
        name="gla_sample",
    )(u, u, u, u, u, state, f_up, f_bias, norm_w, *xarg)
    return y, s


def _head_expand_matrix():
    r = lax.broadcasted_iota(jnp.int32, (128, BR), 0)
    c = lax.broadcasted_iota(jnp.int32, (128, BR), 1)
    return (r == c // HD).astype(F32)


def _stack4(x, mask4):
    return jnp.concatenate([x, x, x, x], axis=0) * mask4


def _quad_masks():
    r = lax.broadcasted_iota(jnp.int32, (4 * CH, QW), 0)
    c = lax.broadcasted_iota(jnp.int32, (4 * CH, QW), 1)
    return (r // CH == c // HD).astype(F32)


def _ssm_prompt_kernel(c, u_ref, cw_ref, cb_ref, dtb_ref, a_ref, d_ref, nw_ref, y_ref, h_ref,
                       carry_ref, x_s, dtl_s, cum_s, lal_s, yo_s):
    @pl.when(c == 0)
    def _():
        h_ref[...] = jnp.zeros_like(h_ref)
        carry_ref[...] = jnp.zeros_like(carry_ref)

    pre = u_ref[:, SS_X:SS_X + SSM_XBC]
    prev = carry_ref[...]
    row8 = lax.broadcasted_iota(jnp.int32, (8, SSM_XBC), 0)
    acc = cb_ref[...] + pre * cw_ref[SSM_CONV - 1:SSM_CONV, :]
    for j in range(1, SSM_CONV):
        sh = pltpu.roll(pre, j, axis=0)
        top = jnp.where(row8 < j, pltpu.roll(prev, j, axis=0), sh[0:8, :])
        sh = jnp.concatenate([top, sh[8:, :]], axis=0)
        acc = acc + sh * cw_ref[SSM_CONV - 1 - j:SSM_CONV - j, :]
    carry_ref[...] = pre[CH - 8:CH, :]
    xbc = _silu(acc)
    x_s[...] = xbc[:, 0:BR]

    expand = _head_expand_matrix()
    tri = _tri_incl(CH)
    dt = _softplus(u_ref[:, SS_DT:SS_DT + 128] + dtb_ref[...])
    la = dt * a_ref[...]
    dtl_s[...] = _xdot(dt, expand)
    lal_s[...] = _xdot(la, expand)
    cum_s[...] = _xdot(_xdot_l(tri, la), expand)

    mask4 = _quad_masks()
    t_i = lax.broadcasted_iota(jnp.int32, (CH, 4 * CH), 0)
    s_i = lax.broadcasted_iota(jnp.int32, (CH, 4 * CH), 1) % CH
    causal = s_i <= t_i
    upto = (t_i <= s_i).astype(F32)
    ones_cn = jnp.ones((CH, SSM_N), F32)
    for q in range(NQ):
        g = q // (NQ // SSM_G)
        ls = slice(q * QW, (q + 1) * QW)
        bm = xbc[:, BR + g * SSM_N:BR + (g + 1) * SSM_N]
        cm = xbc[:, BR + SSM_G * SSM_N + g * SSM_N:BR + SSM_G * SSM_N + (g + 1) * SSM_N]
        cum_q = cum_s[:, ls]
        lal_q = lal_s[:, ls]
        xdt = x_s[:, ls] * dtl_s[:, ls]
        cum_row = jnp.sum(lal_q * upto, axis=0, keepdims=True)
        lmat = jnp.exp(jnp.where(causal, cum_q - cum_row, -1e30))
        cb = _bdot_nt(cm, jnp.concatenate([bm, bm, bm, bm], axis=0))
        y_diag = _bdot(cb * lmat, _stack4(xdt, mask4))
        total = cum_q[CH - 1:CH, :]
        h_prev = h_ref[q * QW:(q + 1) * QW, :]
        y_off = _bdot_nt(cm, h_prev) * jnp.exp(cum_q)
        dec = jnp.exp(_xdot_tn(lal_q, ones_cn))
        h_ref[q * QW:(q + 1) * QW, :] = h_prev * dec + _bdot_tn(xdt * jnp.exp(total - cum_q), bm)
        yo_s[:, ls] = y_diag + y_off + d_ref[:, ls] * x_s[:, ls]

    gw = BR // SSM_G
    for g in range(SSM_G):
        gs = slice(g * gw, (g + 1) * gw)
        yg = yo_s[:, gs] * _silu(u_ref[:, SS_Z + g * gw:SS_Z + (g + 1) * gw])
        ms = jnp.mean(yg * yg, axis=-1, keepdims=True)
        y_ref[:, gs] = (yg * lax.rsqrt(ms + EPS) * nw_ref[:, gs]).astype(y_ref.dtype)


def ssm_prompt(u, n_seq, seq_len, conv_w, conv_b, dt_bias, a_neg, d_lane, norm_w):
    nc, steps, u_map, y_map = _prompt_rows(n_seq, seq_len, u.shape[0])
    const = lambda shape: pl.BlockSpec(shape, lambda b, c: (0, 0))
    y, h = pl.pallas_call(
        _chunk_kernel(_ssm_prompt_kernel, nc, 7),
        grid=(n_seq, steps),
        in_specs=[pl.BlockSpec((CH, SS_PAD), u_map),
                  const((SSM_CONV, SSM_XBC)), const((1, SSM_XBC)), const((1, 128)), const((1, 128)),
                  const((1, BR)), const((1, BR))],
        out_specs=[pl.BlockSpec((CH, BR), y_map),
                   pl.BlockSpec((None, BR, SSM_N), lambda b, c: (b, 0, 0))],
        out_shape=[jax.ShapeDtypeStruct((u.shape[0], BR), BF16),
                   jax.ShapeDtypeStruct((n_seq, BR, SSM_N), F32)],
        scratch_shapes=[pltpu.VMEM((8, SSM_XBC), F32)] + [pltpu.VMEM((CH, BR), F32)] * 5,
        compiler_params=_cparams(("parallel", "arbitrary")),
        name="ssm_prompt",
    )(u, conv_w, conv_b, dt_bias, a_neg, d_lane, norm_w)
    return y, h


def _ssm_sample_kernel(x_ref, b_ref, c_ref, dt_ref, z_ref, sx_ref, sb_ref, sc_ref,
                       cwx_ref, cwb_ref, cwc_ref, cbx_ref, cbb_ref, cbc_ref,
                       dtb_ref, a_ref, d_ref, nw_ref, h_ref, y_ref, ho_ref):
    nb = x_ref.shape[0]
    gw = x_ref.shape[1]

    def conv(cur_ref, st_ref, w_ref, bias_ref):
        acc = bias_ref[...] + cur_ref[...] * w_ref[SSM_CONV - 1:SSM_CONV, :]
        for j in range(SSM_CONV - 1):
            acc = acc + st_ref[j] * w_ref[j:j + 1, :]
        return _silu(acc)

    xs = conv(x_ref, sx_ref, cwx_ref, cbx_ref)
    bm = conv(b_ref, sb_ref, cwb_ref, cbb_ref)
    cm = conv(c_ref, sc_ref, cwc_ref, cbc_ref)
    g = pl.program_id(1)
    r = lax.broadcasted_iota(jnp.int32, (128, gw), 0)
    col = lax.broadcasted_iota(jnp.int32, (128, gw), 1)
    expand = (r == col // HD + g * (gw // HD)).astype(F32)
    dt = _softplus(dt_ref[...] + dtb_ref[...])
    dtl = _hdot(dt, expand)
    decl = jnp.exp(_hdot(dt * a_ref[...], expand))
    xdt = xs * dtl
    rr = lax.broadcasted_iota(jnp.int32, (gw, gw), 0)
    cc = lax.broadcasted_iota(jnp.int32, (gw, gw), 1)
    eye = (rr == cc).astype(F32)
    for b in range(nb):
        dec_col = _col_of_row(decl[b:b + 1, :], eye)
        xdt_col = _col_of_row(xdt[b:b + 1, :], eye)
        h_new = h_ref[b] * dec_col + xdt_col * bm[b:b + 1, :]
        ho_ref[b] = h_new
        y_col = jnp.sum(h_new * cm[b:b + 1, :], axis=-1, keepdims=True)
        y_row = jnp.sum(eye * y_col, axis=0, keepdims=True)
        y_row = (y_row + d_ref[...] * xs[b:b + 1, :]) * _silu(z_ref[b:b + 1, :])
        ms = jnp.mean(y_row * y_row, axis=-1, keepdims=True)
        y_ref[b:b + 1, :] = y_row * lax.rsqrt(ms + EPS) * nw_ref[...]


def ssm_sample(u, row0, n_tok, state, conv_state_t, layer, prev, conv_w, conv_b, dt_bias, a_neg, d_lane, norm_w):
    nb = 8
    kern, xspec, xarg, alias, lead = _state_chain(_ssm_sample_kernel, 19, state.shape[0], prev)
    r0 = row0 // nb
    gw = BR // SSM_G
    bb, cb = BR // SSM_N, (BR + SSM_G * SSM_N) // SSM_N
    ucol = lambda base, w: (lambda i, g: (r0 + i, base // w + g))
    scol = lambda base: (lambda i, g: (0, i, base + g))
    wcol = lambda base: (lambda i, g: (0, base + g))
    y, h = pl.pallas_call(
        kern,
        grid=(n_tok // nb, SSM_G),
        input_output_aliases=alias,
        in_specs=[pl.BlockSpec((nb, gw), ucol(SS_X, gw)),
                  pl.BlockSpec((nb, SSM_N), ucol(SS_B, SSM_N)),
                  pl.BlockSpec((nb, SSM_N), ucol(SS_C, SSM_N)),
                  pl.BlockSpec((nb, 128), lambda i, g: (r0 + i, SS_DT // 128)),
                  pl.BlockSpec((nb, gw), ucol(SS_Z, gw)),
                  pl.BlockSpec((SSM_CONV - 1, nb, gw), lambda i, g: (0, i, g)),
                  pl.BlockSpec((SSM_CONV - 1, nb, SSM_N), scol(bb)),
                  pl.BlockSpec((SSM_CONV - 1, nb, SSM_N), scol(cb)),
                  pl.BlockSpec((SSM_CONV, gw), lambda i, g: (0, g)),
                  pl.BlockSpec((SSM_CONV, SSM_N), wcol(bb)),
                  pl.BlockSpec((SSM_CONV, SSM_N), wcol(cb)),
                  pl.BlockSpec((1, gw), lambda i, g: (0, g)),
                  pl.BlockSpec((1, SSM_N), wcol(bb)),
                  pl.BlockSpec((1, SSM_N), wcol(cb)),
                  pl.BlockSpec((1, 128), lambda i, g: (0, 0)),
                  pl.BlockSpec((1, 128), lambda i, g: (0, 0)),
                  pl.BlockSpec((1, gw), lambda i, g: (0, g)),
                  pl.BlockSpec((1, gw), lambda i, g: (0, g)),
                  pl.BlockSpec((None, nb, None, gw, SSM_N), lambda i, g: (layer, i, g, 0, 0))] + xspec,
        out_specs=[pl.BlockSpec((nb, gw), lambda i, g: (i, g)),
                   pl.BlockSpec((lead, nb, None, gw, SSM_N), lambda i, g: (0 if lead else layer, i, g, 0, 0))],
        out_shape=[jax.ShapeDtypeStruct((n_tok, BR), F32),
                   jax.ShapeDtypeStruct(state.shape, F32)],
        compiler_params=_cparams(("parallel", "parallel")),
        name="ssm_sample",
    )(u, u, u, u, u, conv_state_t, conv_state_t, conv_state_t, conv_w, conv_w, conv_w,
      conv_b, conv_b, conv_b, dt_bias, a_neg, d_lane, norm_w, state, *xarg)
    return y, h


def _rw_mix_inputs(xs_of, w0_ref, w2_ref, a0_ref, a2_ref, g2_ref):
    r = xs_of(RW_R, BR)
    k = xs_of(RW_K, BR)
    v = xs_of(RW_V, BR)
    w = -_softplus(-(w0_ref[...] + _bdot(jnp.tanh(xs_of(RW_WL, 128)), w2_ref[...]))) - 0.5
    log_w = -jnp.exp(w)
    a = _sigmoid(a0_ref[...] + _bdot(xs_of(RW_AL, 128), a2_ref[...]))
    g = _bdot(_sigmoid(xs_of(RW_GL, RW_LG)), g2_ref[...])
    return r, k, v, log_w, a, g


def _seg_sum(x, ones_bd):
    hi = x.astype(BF16)
    lo = (x - hi.astype(F32)).astype(BF16)
    return (jnp.dot(hi, ones_bd, preferred_element_type=F32)
            + jnp.dot(lo, ones_bd, preferred_element_type=F32))


def _rw_prompt_kernel(c, u_ref, mu_ref, w0_ref, w2_ref, a0_ref, a2_ref, g2_ref, kk_ref, ka_ref, rk_ref,
                      lnw_ref, lnb_ref, y_ref, s_ref, carry_ref, r_s, k_s, v_s, lw_s, a_s, g_s):
    @pl.when(c == 0)
    def _():
        s_ref[...] = jnp.zeros_like(s_ref)
        carry_ref[...] = jnp.zeros_like(carry_ref)

    row8 = lax.broadcasted_iota(jnp.int32, (8, 1), 0)

    def xs_of(c0, width):
        cur = u_ref[:, c0:c0 + width]
        sh = pltpu.roll(cur, 1, axis=0)
        top = jnp.where(row8 < 1, pltpu.roll(carry_ref[:, c0:c0 + width], 1, axis=0), sh[0:8, :])
        prev = jnp.concatenate([top, sh[8:, :]], axis=0)
        return cur + (prev - cur) * mu_ref[:, c0:c0 + width]

    r, k, v, log_w, a, g = _rw_mix_inputs(xs_of, w0_ref, w2_ref, a0_ref, a2_ref, g2_ref)
    r_s[...] = r
    k_s[...] = k
    v_s[...] = v
    lw_s[...] = log_w
    a_s[...] = a
    g_s[...] = g
    carry_ref[...] = u_ref[CH - 8:CH, :]

    tri = _tri_incl(CH)
    mask4 = _quad_masks()
    rb = lax.broadcasted_iota(jnp.int32, (QW, QW), 0)
    cb = lax.broadcasted_iota(jnp.int32, (QW, QW), 1)
    bd = (rb // HD == cb // HD)
    bdf = bd.astype(F32)
    ones_bd = bd.astype(BF16)
    t_i = lax.broadcasted_iota(jnp.int32, (CH, 4 * CH), 0)
    s_i = lax.broadcasted_iota(jnp.int32, (CH, 4 * CH), 1) % CH
    strict = (s_i < t_i).astype(F32)
    incl = (s_i <= t_i).astype(F32)
    ident = (s_i == t_i).astype(F32)

    def blockdiag(x):
        return jnp.concatenate([x, x, x, x], axis=0) * bdf

    qs = range(NQ)
    ls = [slice(q * QW, (q + 1) * QW) for q in qs]
    stk = lambda x: _stack4(x, mask4)
    r_q = [r_s[:, ls[q]] for q in qs]
    v_q = [v_s[:, ls[q]] for q in qs]
    a_q = [a_s[:, ls[q]] for q in qs]
    kk = [k_s[:, ls[q]] * kk_ref[:, ls[q]] for q in qs]
    kn = [_seg_sum(kk[q] * kk[q], ones_bd) for q in qs]
    cum = [_xdot_l(tri, lw_s[:, ls[q]]) for q in qs]
    kk = [kk[q] / jnp.maximum(jnp.sqrt(kn[q]), 1e-12) for q in qs]
    kmod = [k_s[:, ls[q]] * (1.0 + (a_q[q] - 1.0) * ka_ref[:, ls[q]]) for q in qs]
    bv = [kk[q] * a_q[q] for q in qs]
    total = [cum[q][CH - 1:CH, :] for q in qs]
    e_neg = [jnp.exp(-cum[q]) for q in qs]
    e_end = [jnp.exp(total[q] - cum[q]) for q in qs]
    at = [-kk[q] * jnp.exp(cum[q] - lw_s[:, ls[q]]) for q in qs]
    rt = [r_q[q] * jnp.exp(cum[q]) for q in qs]
    bbar = [bv[q] * e_end[q] for q in qs]
    kbar = [kmod[q] * e_end[q] for q in qs]

    gram = [_bdot_nt(jnp.concatenate([at[q], rt[q]], axis=0),
                     jnp.concatenate([stk(bv[q] * e_neg[q]), stk(kmod[q] * e_neg[q])], axis=0))
            for q in qs]
    a_ab = [gram[q][0:CH, 0:4 * CH] * strict for q in qs]
    a_ak = [gram[q][0:CH, 4 * CH:8 * CH] * strict for q in qs]
    m_rbk = [jnp.concatenate([gram[q][CH:2 * CH, 0:4 * CH] * incl, gram[q][CH:2 * CH, 4 * CH:8 * CH] * incl],
                             axis=1) for q in qs]

    tinv = [ident + a_ab[q] for q in qs]
    p = a_ab
    for _ in range(5):
        p = [_bdot(p[q], blockdiag(p[q])) for q in qs]
        tinv = [tinv[q] + _bdot(tinv[q], blockdiag(p[q])) for q in qs]

    x1 = [_bdot(a_ak[q], stk(v_q[q])) for q in qs]
    a_hat = [_bdot(tinv[q], stk(at[q])) for q in qs]
    w_v = [_bdot(tinv[q], stk(x1[q])) for q in qs]
    q_hat = [rt[q] + _bdot(m_rbk[q][:, 0:4 * CH], stk(a_hat[q])) for q in qs]
    o_loc = [_bdot(m_rbk[q], jnp.concatenate([stk(w_v[q]), stk(v_q[q])], axis=0)) for q in qs]
    s_prev = [s_ref[q] for q in qs]
    o = [_bdot_nt(q_hat[q], s_prev[q]) + o_loc[q] for q in qs]
    g_corr = [_bdot_tn(a_hat[q], bbar[q]) * bdf for q in qs]
    h_new = [_bdot_tn(jnp.concatenate([w_v[q], v_q[q]], axis=0),
                      jnp.concatenate([bbar[q], kbar[q]], axis=0)) * bdf for q in qs]
    s_corr = [_bdot(s_prev[q], g_corr[q]) for q in qs]
    for q in qs:
        s_ref[q] = s_prev[q] * jnp.exp(total[q]) + s_corr[q] + h_new[q]

    mean = [_seg_sum(o[q], ones_bd) * (1.0 / HD) for q in qs]
    bsum = [_seg_sum(r_q[q] * kmod[q] * rk_ref[:, ls[q]], ones_bd) for q in qs]
    d = [o[q] - mean[q] for q in qs]
    var = [_seg_sum(d[q] * d[q], ones_bd) * (1.0 / HD) for q in qs]
    for q in qs:
        on = d[q] * lax.rsqrt(var[q] + RW_LN_EPS) * lnw_ref[:, ls[q]] + lnb_ref[:, ls[q]]
        y_ref[:, ls[q]] = ((on + bsum[q] * v_q[q]) * g_s[:, ls[q]]).astype(y_ref.dtype)


def rwkv_prompt(u, n_seq, seq_len, mu, w0, w2, a0, a2, g2, k_k, k_a, r_k, ln_w, ln_b):
    nc, steps, u_map, y_map = _prompt_rows(n_seq, seq_len, u.shape[0])
    const = lambda shape: pl.BlockSpec(shape, lambda b, c: (0, 0))
    y, s = pl.pallas_call(
        _chunk_kernel(_rw_prompt_kernel, nc, 12),
        grid=(n_seq, steps),
        in_specs=[pl.BlockSpec((CH, RW_PAD), u_map),
                  const((1, RW_PAD)), const((1, BR)), const((128, BR)), const((1, BR)), const((128, BR)),
                  const((RW_LG, BR)), const((1, BR)), const((1, BR)), const((1, BR)), const((1, BR)),
                  const((1, BR))],
        out_specs=[pl.BlockSpec((CH, BR), y_map),
                   pl.BlockSpec((None, NQ, QW, QW), lambda b, c: (b, 0, 0, 0))],
        out_shape=[jax.ShapeDtypeStruct((u.shape[0], BR), BF16),
                   jax.ShapeDtypeStruct((n_seq, NQ, QW, QW), F32)],
        scratch_shapes=[pltpu.VMEM((8, RW_PAD), F32)] + [pltpu.VMEM((CH, BR), F32)] * 6,
        compiler_params=_cparams(("parallel", "arbitrary")),
        name="rwkv_prompt",
    )(u, mu, w0, w2, a0, a2, g2, k_k, k_a, r_k, ln_w, ln_b)
    return y, s


def _rw_sample_pre_kernel(u_ref, sh_ref, mu_ref, w0_ref, w2_ref, a0_ref, a2_ref, g2_ref,
                          r_o, k_o, v_o, w_o, a_o, g_o):
    def xs_of(c0, width):
        cur = u_ref[:, c0:c0 + width]
        return cur + (sh_ref[:, c0:c0 + width] - cur) * mu_ref[:, c0:c0 + width]

    r, k, v, log_w, a, g = _rw_mix_inputs(xs_of, w0_ref, w2_ref, a0_ref, a2_ref, g2_ref)
    r_o[...] = r
    k_o[...] = k
    v_o[...] = v
    w_o[...] = jnp.exp(log_w)
    a_o[...] = a
    g_o[...] = g


def _rw_sample_step_kernel(r_ref, k_ref, v_ref, w_ref, a_ref, g_ref, kk_ref, ka_ref, rk_ref,
                           lnw_ref, lnb_ref, s_ref, y_ref, so_ref, o_scr):
    for h in range(r_ref.shape[0]):
        r, k, a = r_ref[h], k_ref[h], a_ref[h]
        w = w_ref[h]
        kk = k * kk_ref[h]
        kk = kk / jnp.maximum(jnp.sqrt(jnp.sum(kk * kk, axis=0, keepdims=True)), 1e-12)
        kmod = k * (1.0 + (a - 1.0) * ka_ref[h])
        av = -kk
        bv = kk * a

        def body(vi, carry):
            s = s_ref[h, vi]
            sa = jnp.sum(s * av, axis=0, keepdims=True)
            s_new = s * w + sa * bv + v_ref[h, pl.ds(vi, 1), :] * kmod
            so_ref[h, vi] = s_new
            o_scr[pl.ds(vi, 1), :] = jnp.sum(s_new * r, axis=0, keepdims=True)
            return carry

        lax.fori_loop(0, HD, body, 0)
        o = o_scr[...]
        mean = jnp.mean(o, axis=0, keepdims=True)
        d = o - mean
        var = jnp.mean(d * d, axis=0, keepdims=True)
        on = d * lax.rsqrt(var + RW_LN_EPS) * lnw_ref[h] + lnb_ref[h]
        bonus = jnp.sum(r * kmod * rk_ref[h], axis=0, keepdims=True) * v_ref[h]
        y_ref[h] = (on + bonus) * g_ref[h]


def rwkv_sample(u, row0, n_tok, state_t, shift_prev, layer, prev, mu, w0, w2, a0, a2, g2, k_k, k_a, r_k, ln_w, ln_b):
    const = lambda shape: pl.BlockSpec(shape, lambda i: (0, 0))
    vec = jax.ShapeDtypeStruct((n_tok, BR), F32)
    outs = pl.pallas_call(
        _rw_sample_pre_kernel,
        grid=(1,),
        in_specs=[pl.BlockSpec((n_tok, RW_PAD), lambda i: (row0 // n_tok, 0)),
                  pl.BlockSpec((None, n_tok, RW_PAD), lambda i: (layer, 0, 0)),
                  const((1, RW_PAD)), const((1, BR)), const((128, BR)), const((1, BR)), const((128, BR)),
                  const((RW_LG, BR))],
        out_specs=[pl.BlockSpec((n_tok, BR), lambda i: (0, 0))] * 6,
        out_shape=[vec] * 6,
        compiler_params=_cparams(("arbitrary",)),
        name="rwkv_sample_pre",
    )(u, shift_prev, mu, w0, w2, a0, a2, g2)
    outs = lax.optimization_barrier(outs)
    hb = 2
    vspec = pl.BlockSpec((hb, HD, n_tok), lambda i: (i, 0, 0))
    pspec = pl.BlockSpec((hb, HD, 1), lambda i: (i, 0, 0))
    kern, xspec, xarg, alias, lead = _state_chain(_rw_sample_step_kernel, 12, state_t.shape[0], prev)
    y, s = pl.pallas_call(
        kern,
        grid=(NH // hb,),
        input_output_aliases=alias,
        in_specs=[vspec] * 6 + [pspec] * 5
        + [pl.BlockSpec((None, hb, HD, HD, n_tok), lambda i: (layer, i, 0, 0, 0))] + xspec,
        out_specs=[vspec, pl.BlockSpec((lead, hb, HD, HD, n_tok), lambda i: (0 if lead else layer, i, 0, 0, 0))],
        out_shape=[jax.ShapeDtypeStruct((NH, HD, n_tok), F32),
                   jax.ShapeDtypeStruct(state_t.shape, F32)],
        scratch_shapes=[pltpu.VMEM((HD, n_tok), F32)],
        compiler_params=_cparams(("parallel",)),
        name="rwkv_sample_step",
    )(*[t.T.reshape(NH, HD, n_tok) for t in outs],
      *[t.reshape(NH, HD, 1) for t in (k_k, k_a, r_k, ln_w, ln_b)], state_t, *xarg)
    return y.reshape(BR, n_tok).T, s


def _pad_last(t, width):
    return jnp.pad(t, [(0, 0)] * (t.ndim - 1) + [(0, width - t.shape[-1])])


def _rw_to_padded(t):
    r_k_v = t[..., :3 * BR]
    wl = t[..., 3 * BR:3 * BR + RW_LW]
    al = t[..., 3 * BR + RW_LW:3 * BR + RW_LW + RW_LA]
    gl = t[..., 3 * BR + RW_LW + RW_LA:]
    return jnp.concatenate([r_k_v, _pad_last(wl, 128), _pad_last(al, 128), gl], axis=-1)


def _rw_from_padded(t):
    return jnp.concatenate([t[..., :RW_WL], t[..., RW_WL:RW_WL + RW_LW], t[..., RW_AL:RW_AL + RW_LA],
                            t[..., RW_GL:RW_GL + RW_LG]], axis=-1)


def _split_w_in(w_in):
    o1 = RW_COLS
    o2 = o1 + SS_COLS
    o3 = o2 + GL_COLS
    wt = jnp.transpose(w_in, (0, 2, 1)).astype(BF16)
    rows = lambda a, b: wt[:, a:b, :]
    zeros = lambda n: jnp.zeros((wt.shape[0], n, wt.shape[2]), BF16)
    cat = lambda parts: jnp.concatenate(parts, axis=1)
    w_rw = cat([rows(0, 3 * BR + RW_LW), zeros(128 - RW_LW),
                rows(3 * BR + RW_LW, 3 * BR + RW_LW + RW_LA), zeros(128 - RW_LA), rows(3 * BR + RW_LW + RW_LA, o1)])
    w_ss = cat([rows(o1, o2), zeros(SS_PAD - SS_COLS)])
    w_gl = cat([rows(o2, o3), zeros(GL_PAD - GL_COLS)])
    return w_rw, w_ss, w_gl, rows(o3, w_in.shape[2])


def _row(v):
    return v.reshape(1, -1).astype(F32)


def kernel(x_prompt, x_sample, p_prompt, p_sample, state_rwkv, state_shift, state_ssm, state_conv, state_gla, norm_mix, w_in, rw_mu, rw_w0, rw_w2, rw_a0, rw_a2, rw_g2, rw_kk, rw_ka, rw_rk, rw_ln_w, rw_ln_b, ssm_conv_w, ssm_conv_b, ssm_dt_bias, ssm_a_log, ssm_d, ssm_norm, gla_f_up, gla_f_bias, gla_norm, w_branch, w_out, norm_ffn, w_ff1, w_ff2, norm_ple, w_ple_gate, w_ple_proj, norm_final):
    depth = w_in.shape[0]
    bp, seq, d = x_prompt.shape
    ns = x_sample.shape[0]
    assert x_sample.shape[1] == 1, "the sample group is decoded one token at a time"
    assert seq % CH == 0 and seq >= 8
    tp = bp * seq
    assert tp % ns == 0 and ns % 8 == 0

    x = jnp.concatenate([x_prompt.reshape(tp, d), x_sample.reshape(ns, d)], axis=0)
    ple = jnp.concatenate([p_prompt.reshape(depth, tp, -1), p_sample.reshape(depth, ns, -1)], axis=1).astype(BF16)
    shift_pad = _rw_to_padded(state_shift)
    rw_state_t = jnp.transpose(state_rwkv, (0, 2, 3, 4, 1))
    ssm_state = state_ssm.reshape(depth, ns, SSM_G, BR // SSM_G, SSM_N)
    conv_t = jnp.transpose(state_conv, (0, 2, 1, 3))
    wb_branch, wb_ff2 = w_branch.astype(BF16), w_ff2.astype(BF16)

    outs = {k: [] for k in ("rw_p", "sh_p", "sh_s", "ss_p", "cv_p", "cv_s", "gl_p")}
    s_rw = s_ss = s_gl = None
    h = rms_norm(x, norm_mix[0], BF16)
    wb_rw, wb_ss, wb_gl, wb_gates = _split_w_in(w_in)
    for i in range(depth):
        u_rw = project(h, wb_rw, i)
        u_ss = project(h, wb_ss, i)
        u_gl = project(h, wb_gl, i)
        u_gate = project(h, wb_gates, i)

        rw_par = (_rw_to_padded(_row(rw_mu[i])), _row(rw_w0[i]),
                  jnp.pad(rw_w2[i], ((0, 128 - RW_LW), (0, 0))).astype(BF16), _row(rw_a0[i]),
                  jnp.pad(rw_a2[i], ((0, 128 - RW_LA), (0, 0))).astype(BF16), rw_g2[i].astype(BF16),
                  _row(rw_kk[i]), _row(rw_ka[i]), _row(rw_rk[i]), _row(rw_ln_w[i]), _row(rw_ln_b[i]))
        y_rw_p, s_bd = rwkv_prompt(u_rw, bp, seq, *rw_par)
        y_rw_s, s_rw = rwkv_sample(u_rw, tp, ns, rw_state_t, shift_pad, i, s_rw, *rw_par)
        sb = s_bd.reshape(bp, NQ, 4, HD, 4, HD)
        outs["rw_p"].append(jnp.stack([sb[:, :, j, :, j, :] for j in range(4)], axis=2).reshape(bp, NH, HD, HD))
        outs["sh_p"].append(_rw_from_padded(u_rw[seq - 1:tp:seq]))
        outs["sh_s"].append(_rw_from_padded(u_rw[tp:]))

        pad128 = lambda v: jnp.pad(v.astype(F32), (0, 128 - v.shape[0])).reshape(1, 128)
        ss_par = (ssm_conv_w[i], _row(ssm_conv_b[i]), pad128(ssm_dt_bias[i]),
                  pad128(-jnp.exp(ssm_a_log[i].astype(F32))), _row(jnp.repeat(ssm_d[i], HD)), _row(ssm_norm[i]))
        y_ss_p, h_p = ssm_prompt(u_ss, bp, seq, *ss_par)
        y_ss_s, s_ss = ssm_sample(u_ss, tp, ns, ssm_state, conv_t[i], i, s_ss, *ss_par)
        outs["ss_p"].append(h_p.reshape(bp, NH, HD, SSM_N))
        outs["cv_p"].append(jnp.stack([u_ss[(b + 1) * seq - (SSM_CONV - 1):(b + 1) * seq, SS_X:SS_X + SSM_XBC]
                                       for b in range(bp)]))
        outs["cv_s"].append(jnp.concatenate([state_conv[i][:, 1:], u_ss[tp:, None, SS_X:SS_X + SSM_XBC]], axis=1))

        gl_par = (jnp.pad(gla_f_up[i], ((0, 128 - GLA_LORA), (0, 0))).astype(BF16), _row(gla_f_bias[i]),
                  _row(gla_norm[i]))
        y_gl_p, g_p = gla_prompt(u_gl, bp, seq, *gl_par)
        y_gl_s, s_gl = gla_sample(u_gl, tp, ns, state_gla, i, s_gl, *gl_par)
        outs["gl_p"].append(g_p)

        ys = [insert_rows(yp, ysm, tp) for yp, ysm in ((y_rw_p, y_rw_s), (y_ss_p, y_ss_s), (y_gl_p, y_gl_s))]
        merged = merge_branches(ys, wb_branch, i, u_gate)
        x = residual_matmul(merged, w_out, i, x)
        hf = rms_norm(x, norm_ffn[i], BF16)
        x = ffn_down(ffn_up(hf, w_ff1, i), wb_ff2, i, x)
        hn = rms_norm(x, norm_ple[i], BF16)
        x = ple_update(ple, w_ple_proj, hn, w_ple_gate, i, x)
        if i + 1 < depth:
            h = rms_norm(x, norm_mix[i + 1], BF16)

    y_p, y_s = rms_norm_split(x, norm_final, tp, ns)
    st = lambda k: jnp.stack(outs[k])
    return (y_p.reshape(bp, seq, d), y_s.reshape(ns, 1, d), st("rw_p"),
            jnp.transpose(s_rw, (0, 4, 1, 2, 3)), st("sh_p"), st("sh_s"),
            st("ss_p"), s_ss.reshape(depth, ns, NH, HD, SSM_N), st("cv_p"), st("cv_s"), st("gl_p"), s_gl)
```
